```python
import math
import jax, jax.numpy as jnp
from jax import lax
import numpy as np

D_MODEL = 1024
BATCH = 4
SEQ = 4096
DEPTH = 1
DEC_BATCH = 32
DEC_SEQ = 1
PAST_LEN = 16384
PAGE_SIZE = 128

RET_HEADS = 4
RET_DK = 128
RET_DV = 128
RET_WIDTH = RET_HEADS * RET_DV
RET_CHUNK = 128
ROPE_BASE = 10000.0
DIFF_HEADS = 4
DIFF_DH = 64
DIFF_DV = 2 * DIFF_DH
DIFF_WIDTH = DIFF_HEADS * DIFF_DV
DIFF_SCALE = DIFF_DH ** -0.5
Q_BLOCK = 128
MIX_WIDTH = RET_WIDTH + DIFF_WIDTH
IN_SIZES = (RET_HEADS * RET_DK, RET_HEADS * RET_DK, RET_WIDTH, RET_WIDTH,
            DIFF_HEADS * 2 * DIFF_DH, DIFF_HEADS * 2 * DIFF_DH, DIFF_WIDTH)
IN_WIDTH = sum(IN_SIZES)
IN_SPLITS = tuple(int(v) for v in np.cumsum(IN_SIZES)[:-1])
N_BUCKETS = 32
MAX_DISTANCE = 128
MEM_LEN = 256
MEM_HEADS = 4
MEM_DH = D_MODEL // MEM_HEADS
PEER_HEADS = 8
N_KEYS = 128
N_EXPERTS = N_KEYS * N_KEYS
PEER_TOPK = 16
PEER_DQ = 256
PEER_DHALF = PEER_DQ // 2
PEER_BLOCK = 128
PEER_V_SCALE = 0.2

EPS = 1e-6
NEG_INF = -1e30

kernel_name = "hybrid_retention_diffattn_peer_step"


def rms_norm(x, g):
    xf = x.astype(jnp.float32)
    y = xf * lax.rsqrt(jnp.mean(xf * xf, axis=-1, keepdims=True) + EPS)
    return (y * g.astype(jnp.float32)).astype(x.dtype)


def head_norm(x):
    return x * lax.rsqrt(jnp.mean(x * x, axis=-1, keepdims=True) + EPS)


def rotary(x, pos):
    half = x.shape[-1] // 2
    inv = 1.0 / (ROPE_BASE ** (jnp.arange(half, dtype=jnp.float32) / half))
    ang = pos.astype(jnp.float32)[:, None] * inv[None, :]
    cos = jnp.cos(ang)[:, None, :]
    sin = jnp.sin(ang)[:, None, :]
    x1, x2 = x[..., :half], x[..., half:]
    return jnp.concatenate([x1 * cos - x2 * sin, x1 * sin + x2 * cos], axis=-1)


def t5_bias(rel, table):
    n = jnp.maximum(rel, 0)
    max_exact = N_BUCKETS // 2
    nf = jnp.maximum(n, 1).astype(jnp.float32)
    large = max_exact + (jnp.log(nf / max_exact) / math.log(MAX_DISTANCE / max_exact)
                         * (N_BUCKETS - max_exact)).astype(jnp.int32)
    large = jnp.minimum(large, N_BUCKETS - 1)
    bucket = jnp.where(n < max_exact, n, large)
    return jnp.moveaxis(table[bucket].astype(jnp.float32), -1, 0)


def token_mix_inputs(x, g_mix, w_in, pos):
    B, T, _ = x.shape
    xn = rms_norm(x, g_mix)
    proj = (xn @ w_in).astype(jnp.float32)
    rq, rk, rv, rg, dq, dk, dv = jnp.split(proj, IN_SPLITS, axis=-1)
    rq = rotary(rq.reshape(B, T, RET_HEADS, RET_DK), pos) * (RET_DK ** -0.5)
    rk = rotary(rk.reshape(B, T, RET_HEADS, RET_DK), pos)
    rv = rv.reshape(B, T, RET_HEADS, RET_DV)
    dq = dq.reshape(B, T, DIFF_HEADS, 2, DIFF_DH)
    dk = dk.reshape(B, T, DIFF_HEADS, 2, DIFF_DH)
    dv = dv.reshape(B, T, DIFF_HEADS, DIFF_DV)
    return rq, rk, rv, rg, dq, dk, dv


def retention_chunk(state, q, k, v, log_gamma):
    L = q.shape[1]
    n = jnp.arange(L, dtype=jnp.float32)
    diff = n[:, None] - n[None, :]
    decay = jnp.where(diff[None] >= 0,
                      jnp.exp(jnp.maximum(diff, 0.0)[None] * log_gamma[:, None, None]), 0.0)
    scores = jnp.einsum('bnhd,bmhd->bhnm', q, k) * decay[None]
    inner = jnp.einsum('bhnm,bmhe->bnhe', scores, v)
    q_decay = jnp.exp((n + 1.0)[:, None] * log_gamma[None, :])
    cross = jnp.einsum('bnhd,bhde->bnhe', q, state) * q_decay[None, :, :, None]
    k_decay = jnp.exp((L - 1.0 - n)[None, :] * log_gamma[:, None])
    new_state = (jnp.exp(L * log_gamma)[None, :, None, None] * state
                 + jnp.einsum('bmhd,hm,bmhe->bhde', k, k_decay, v))
    return inner + cross, new_state


def retention_prompt(rq, rk, rv, log_gamma):
    B, S = rq.shape[:2]
    nc = S // RET_CHUNK

    def to_chunks(t):
        return jnp.moveaxis(t.reshape(B, nc, RET_CHUNK, *t.shape[2:]), 1, 0)

    s0 = jnp.zeros((B, RET_HEADS, RET_DK, RET_DV), jnp.float32)

    def step(state, qkv):
        o, state = retention_chunk(state, qkv[0], qkv[1], qkv[2], log_gamma)
        return state, o

    s_fin, o = lax.scan(step, s0, (to_chunks(rq), to_chunks(rk), to_chunks(rv)))
    return jnp.moveaxis(o, 0, 1).reshape(B, S, RET_HEADS, RET_DV), s_fin


def diff_lambda(lq1, lk1, lq2, lk2, lam_init):
    f = jnp.float32
    return (jnp.exp(jnp.sum(lq1.astype(f) * lk1.astype(f)))
            - jnp.exp(jnp.sum(lq2.astype(f) * lk2.astype(f))) + lam_init)


def diff_weights(s, lam):
    p = jax.nn.softmax(s.astype(jnp.float32), axis=-1)
    return p[:, :, 0] - lam * p[:, :, 1]


def diff_attn_prompt(dq, dk, dv, lam, rel_bias):
    B, S = dq.shape[:2]
    nb = S // Q_BLOCK
    qb = jnp.moveaxis(dq.reshape(B, nb, Q_BLOCK, DIFF_HEADS, 2, DIFF_DH), 1, 0)
    kpos = jnp.arange(S)

    def block(args):
        i, q = args
        qpos = i * Q_BLOCK + jnp.arange(Q_BLOCK)
        rel = qpos[:, None] - kpos[None, :]
        s = jnp.einsum('bqhcd,bkhcd->bhcqk', q, dk) * DIFF_SCALE + t5_bias(rel, rel_bias)[:, None]
        s = jnp.where(rel >= 0, s, NEG_INF)
        w = diff_weights(s, lam)
        return jnp.einsum('bhqk,bkhe->bqhe', w, dv)

    o = lax.map(block, (jnp.arange(nb), qb))
    return jnp.moveaxis(o, 0, 1).reshape(B, S, DIFF_HEADS, DIFF_DV)


def diff_attn_sample(dq, dk, dv, k_past, v_past, lam, rel_bias):
    T = dq.shape[1]
    P = k_past.shape[1]
    qpos = P + jnp.arange(T)
    rel_past = qpos[:, None] - jnp.arange(P)[None, :]
    rel_new = qpos[:, None] - qpos[None, :]
    s_past = jnp.einsum('bqhcd,bkhcd->bhcqk', dq, k_past) * DIFF_SCALE + t5_bias(rel_past, rel_bias)[:, None]
    s_new = jnp.einsum('bqhcd,bkhcd->bhcqk', dq, dk) * DIFF_SCALE + t5_bias(rel_new, rel_bias)[:, None]
    s_new = jnp.where(rel_new >= 0, s_new, NEG_INF)
    w = diff_weights(jnp.concatenate([s_past, s_new], axis=-1), lam)
    return (jnp.einsum('bhqk,bkhe->bqhe', w[..., :P], v_past)
            + jnp.einsum('bhqk,bkhe->bqhe', w[..., P:], dv))


def mix_output(ro, rg, do, lam_init, w_out):
    B, T = ro.shape[:2]
    ret = head_norm(ro).reshape(B, T, RET_WIDTH) * jax.nn.silu(rg)
    dif = head_norm(do).reshape(B, T, DIFF_WIDTH) * (1.0 - lam_init)
    return jnp.concatenate([ret, dif], axis=-1).astype(w_out.dtype) @ w_out


def mem_kv(mem, g_mem, w_mk, w_mv):
    B, M, _ = mem.shape
    mn = rms_norm(mem, g_mem)
    return ((mn @ w_mk).reshape(B, M, MEM_HEADS, MEM_DH),
            (mn @ w_mv).reshape(B, M, MEM_HEADS, MEM_DH))


def cross_attend(x, g_cross, w_mq, w_mo, mk, mv):
    B, T, D = x.shape
    q = (rms_norm(x, g_cross) @ w_mq).reshape(B, T, MEM_HEADS, MEM_DH)
    s = jnp.einsum('bqhd,bkhd->bhqk', q, mk).astype(jnp.float32) * (MEM_DH ** -0.5)
    p = jax.nn.softmax(s, axis=-1)
    o = jnp.einsum('bhqk,bkhd->bqhd', p, mv.astype(jnp.float32)).reshape(B, T, D)
    return o.astype(w_mo.dtype) @ w_mo


def peer_rows(xb, w_pq, sub_keys, peer_u, peer_v):
    n = xb.shape[0]
    q = (xb @ w_pq).astype(jnp.float32).reshape(n, PEER_HEADS, 2, PEER_DHALF)
    s = jnp.einsum('nhcd,hckd->nhck', q, sub_keys.astype(jnp.float32))
    top_s, top_i = lax.top_k(s, PEER_TOPK)
    cand_s = (top_s[:, :, 0, :, None] + top_s[:, :, 1, None, :]).reshape(n, PEER_HEADS, PEER_TOPK * PEER_TOPK)
    cand_i = (top_i[:, :, 0, :, None] * N_KEYS + top_i[:, :, 1, None, :]).reshape(n, PEER_HEADS, PEER_TOPK * PEER_TOPK)
    best_s, best_pos = lax.top_k(cand_s, PEER_TOPK)
    idx = jnp.take_along_axis(cand_i, best_pos, axis=-1)
    g = jax.nn.softmax(best_s, axis=-1)
    u = peer_u[idx]
    a = jax.nn.gelu(jnp.einsum('nd,nhkd->nhk', xb, u).astype(jnp.float32), approximate=False)
    v = peer_v[idx]
    return jnp.einsum('nhk,nhkd->nd', g * a, v.astype(jnp.float32)).astype(xb.dtype)


def peer_ffn(xn, w_pq, sub_keys, peer_u, peer_v):
    B, T, D = xn.shape
    n = B * T
    flat = xn.reshape(n, D)
    if n % PEER_BLOCK == 0 and n > PEER_BLOCK:
        out = lax.map(lambda xb: peer_rows(xb, w_pq, sub_keys, peer_u, peer_v),
                      flat.reshape(n // PEER_BLOCK, PEER_BLOCK, D)).reshape(n, D)
    else:
        out = peer_rows(flat, w_pq, sub_keys, peer_u, peer_v)
    return out.reshape(B, T, D)


def setup_inputs(seed: int = 0) -> dict:
    key = jax.random.key(seed)
    ks = jax.random.split(key, 32)
    f32 = jnp.float32
    n_pages = PAST_LEN // PAGE_SIZE
    n_used = DEC_BATCH * n_pages
    p_total = n_used + n_used // 4

    def nrm(k, shape, scale):
        return jax.random.normal(k, shape, f32) * scale

    def gain(k, shape):
        return 1.0 + 0.05 * jax.random.normal(k, shape, f32)

    page_table = jax.random.permutation(ks[5], p_total)[:n_used].reshape(DEC_BATCH, n_pages).astype(jnp.int32)
    return {
        "x_prompt": nrm(ks[0], (BATCH, SEQ, D_MODEL), 1.0),
        "x_sample": nrm(ks[1], (DEC_BATCH, DEC_SEQ, D_MODEL), 1.0),
        "mem_prompt": nrm(ks[2], (BATCH, MEM_LEN, D_MODEL), 1.0),
        "cache_diff_k": nrm(ks[3], (DEPTH, p_total, PAGE_SIZE, DIFF_HEADS, 2, DIFF_DH), 1.0),
        "cache_diff_v": nrm(ks[4], (DEPTH, p_total, PAGE_SIZE, DIFF_HEADS, DIFF_DV), 1.0),
        "page_table": page_table,
        "state_ret": nrm(ks[6], (DEPTH, DEC_BATCH, RET_HEADS, RET_DK, RET_DV), 1.0),
        "cache_mem_k": nrm(ks[7], (DEPTH, DEC_BATCH, MEM_LEN, MEM_HEADS, MEM_DH), 1.0),
        "cache_mem_v": nrm(ks[8], (DEPTH, DEC_BATCH, MEM_LEN, MEM_HEADS, MEM_DH), 1.0),
        "g_mix": gain(ks[9], (DEPTH, D_MODEL)),
        "w_in": nrm(ks[10], (DEPTH, D_MODEL, IN_WIDTH), D_MODEL ** -0.5),
        "w_out": nrm(ks[11], (DEPTH, MIX_WIDTH, D_MODEL), MIX_WIDTH ** -0.5),
        "lam_q1": nrm(ks[12], (DEPTH, DIFF_DH), 0.1),
        "lam_k1": nrm(ks[13], (DEPTH, DIFF_DH), 0.1),
        "lam_q2": nrm(ks[14], (DEPTH, DIFF_DH), 0.1),
        "lam_k2": nrm(ks[15], (DEPTH, DIFF_DH), 0.1),
        "rel_bias": nrm(ks[16], (N_BUCKETS, DIFF_HEADS), 0.5),
        "g_cross": gain(ks[17], (DEPTH, D_MODEL)),
        "g_mem": gain(ks[18], (DEPTH, D_MODEL)),
        "w_mq": nrm(ks[19], (DEPTH, D_MODEL, D_MODEL), D_MODEL ** -0.5),
        "w_mk": nrm(ks[20], (DEPTH, D_MODEL, D_MODEL), D_MODEL ** -0.5),
        "w_mv": nrm(ks[21], (DEPTH, D_MODEL, D_MODEL), D_MODEL ** -0.5),
        "w_mo": nrm(ks[22], (DEPTH, D_MODEL, D_MODEL), D_MODEL ** -0.5),
        "g_ffn": gain(ks[23], (DEPTH, D_MODEL)),
        "w_pq": nrm(ks[24], (DEPTH, D_MODEL, PEER_HEADS * PEER_DQ), D_MODEL ** -0.5),
        "sub_keys": nrm(ks[25], (DEPTH, PEER_HEADS, 2, N_KEYS, PEER_DHALF), PEER_DHALF ** -0.5),
        "peer_u": nrm(ks[26], (DEPTH, N_EXPERTS, D_MODEL), D_MODEL ** -0.5),
        "peer_v": nrm(ks[27], (DEPTH, N_EXPERTS, D_MODEL), PEER_V_SCALE),
        "g_final": gain(ks[28], (D_MODEL,)),
    }


def reference(x_prompt, x_sample, mem_prompt, cache_diff_k, cache_diff_v, page_table, state_ret,
              cache_mem_k, cache_mem_v, g_mix, w_in, w_out, lam_q1, lam_k1, lam_q2, lam_k2, rel_bias,
              g_cross, g_mem, w_mq, w_mk, w_mv, w_mo, g_ffn, w_pq, sub_keys, peer_u, peer_v, g_final):
    n_pages = page_table.shape[1]
    past = n_pages * PAGE_SIZE
    db, t_s = x_sample.shape[:2]
    log_gamma = jnp.log(1.0 - 2.0 ** (-5.0 - jnp.arange(RET_HEADS, dtype=jnp.float32)))
    pos_p = jnp.arange(x_prompt.shape[1])
    pos_s = past + jnp.arange(t_s)

    hp, hs = x_prompt, x_sample
    k_p_list, v_p_list, ret_p_list, mk_p_list, mv_p_list = [], [], [], [], []
    k_s_list, v_s_list, ret_s_list = [], [], []
    for l in range(DEPTH):
        lam_init = 0.8 - 0.6 * math.exp(-0.3 * l)
        lam = diff_lambda(lam_q1[l], lam_k1[l], lam_q2[l], lam_k2[l], lam_init)

        rq, rk, rv, rg, dq, dk, dv = token_mix_inputs(hp, g_mix[l], w_in[l], pos_p)
        ro, ret_state_p = retention_prompt(rq, rk, rv, log_gamma)
        do = diff_attn_prompt(dq, dk, dv, lam, rel_bias)
        hp = hp + mix_output(ro, rg, do, lam_init, w_out[l]).astype(hp.dtype)
        mk, mv = mem_kv(mem_prompt, g_mem[l], w_mk[l], w_mv[l])
        hp = hp + cross_attend(hp, g_cross[l], w_mq[l], w_mo[l], mk, mv).astype(hp.dtype)
        hp = hp + peer_ffn(rms_norm(hp, g_ffn[l]), w_pq[l], sub_keys[l], peer_u[l], peer_v[l]).astype(hp.dtype)
        k_p_list.append(dk.astype(cache_diff_k.dtype))
        v_p_list.append(dv.astype(cache_diff_v.dtype))
        ret_p_list.append(ret_state_p.astype(state_ret.dtype))
        mk_p_list.append(mk.astype(cache_mem_k.dtype))
        mv_p_list.append(mv.astype(cache_mem_v.dtype))

        rq, rk, rv, rg, dq, dk, dv = token_mix_inputs(hs, g_mix[l], w_in[l], pos_s)
        ro_s, ret_state_s = retention_chunk(state_ret[l].astype(jnp.float32), rq, rk, rv, log_gamma)
        k_past = cache_diff_k[l][page_table].reshape(db, past, DIFF_HEADS, 2, DIFF_DH)
        v_past = cache_diff_v[l][page_table].reshape(db, past, DIFF_HEADS, DIFF_DV)
        do_s = diff_attn_sample(dq, dk, dv, k_past, v_past, lam, rel_bias)
        hs = hs + mix_output(ro_s, rg, do_s, lam_init, w_out[l]).astype(hs.dtype)
        hs = hs + cross_attend(hs, g_cross[l], w_mq[l], w_mo[l], cache_mem_k[l], cache_mem_v[l]).astype(hs.dtype)
        hs = hs + peer_ffn(rms_norm(hs, g_ffn[l]), w_pq[l], sub_keys[l], peer_u[l], peer_v[l]).astype(hs.dtype)
        k_s_list.append(dk.astype(cache_diff_k.dtype))
        v_s_list.append(dv.astype(cache_diff_v.dtype))
        ret_s_list.append(ret_state_s.astype(state_ret.dtype))

    y_prompt = rms_norm(hp, g_final)
    y_sample = rms_norm(hs, g_final)
    new_diff_k_prompt = jnp.stack(k_p_list)
    new_diff_v_prompt = jnp.stack(v_p_list)
    state_ret_prompt = jnp.stack(ret_p_list)
    mem_k_prompt = jnp.stack(mk_p_list)
    mem_v_prompt = jnp.stack(mv_p_list)
    new_diff_k_sample = jnp.stack(k_s_list)
    new_diff_v_sample = jnp.stack(v_s_list)
    state_ret_sample = jnp.stack(ret_s_list)
    return (y_prompt, y_sample, new_diff_k_prompt, new_diff_v_prompt, state_ret_prompt,
            mem_k_prompt, mem_v_prompt, new_diff_k_sample, new_diff_v_sample, state_ret_sample)
```

```python
import functools
import math

import jax
import jax.numpy as jnp
import numpy as np
from jax import lax
from jax.experimental import pallas as pl
from jax.experimental.pallas import tpu as pltpu

F32 = jnp.float32
BF16 = jnp.bfloat16

EPS = 1e-6
NEG_INF = -1e30
ROPE_BASE = 10000.0
RET_HEADS = 4
RET_DK = 128
RET_CHUNK = 128
DIFF_HEADS = 4
DIFF_DH = 64
HEAD_W = 128
GROUP_W = 512
N_GROUPS = 7
PAGE = 128
N_BUCKETS = 32
MAX_DISTANCE = 128
MEM_HEADS = 4
PEER_HEADS = 8
N_KEYS = 128
PEER_TOPK = 16
PEER_DQ = 256

VMEM_LIMIT = 56 * 1024 * 1024
ATTN_TILE = 512
TOKEN_TILE = 512
PEER_EXPERT_TILE = 1024
ROUTE_TILE = 128
PAGES_PER_STEP = 8


def _params(semantics, vmem=VMEM_LIMIT):
    return pltpu.CompilerParams(dimension_semantics=semantics, vmem_limit_bytes=vmem)


def _rms(x, g):
    return (x * lax.rsqrt(jnp.mean(x * x, axis=-1, keepdims=True) + EPS)) * g


def _dot(a, b):
    return jnp.dot(a, b, preferred_element_type=F32)


def _dot_nt(a, b):
    return lax.dot_general(a, b, (((1,), (1,)), ((), ())), preferred_element_type=F32)


def _in_proj_kernel(x_ref, g_ref, w_ref, cs_ref, sn_ref,
                    rq_ref, rk_ref, rv_ref, rg_ref, dqb_ref, dk_ref, dv_ref, dkb_ref, dvb_ref):
    xb = _rms(x_ref[...], g_ref[...]).astype(BF16)
    cs = cs_ref[...]
    sn = sn_ref[...]

    def proj(c):
        return _dot(xb, w_ref[:, c * GROUP_W:(c + 1) * GROUP_W])

    def rotary(p, out_ref, scale):
        for h in range(RET_HEADS):
            ph = p[:, h * HEAD_W:(h + 1) * HEAD_W]
            r = ph * cs + pltpu.roll(ph, HEAD_W // 2, 1) * sn
            if scale is not None:
                r = r * scale
            out_ref[:, h * HEAD_W:(h + 1) * HEAD_W] = r

    rotary(proj(0), rq_ref, RET_DK ** -0.5)
    rotary(proj(1), rk_ref, None)
    rv_ref[...] = proj(2)
    rg_ref[...] = proj(3)
    dqb_ref[...] = (proj(4) * (DIFF_DH ** -0.5)).astype(BF16)
    dk = proj(5)
    dk_ref[...] = dk
    dkb_ref[...] = dk.astype(BF16)
    dv = proj(6)
    dv_ref[...] = dv
    dvb_ref[...] = dv.astype(BF16)


def _in_proj(x, g, w_bf, cs, sn, tm):
    n, d = x.shape
    nt = n // tm
    n_pos = cs.shape[0] // tm
    row = lambda i: (i, 0)
    pos = lambda i: (i % n_pos, 0)
    fix = lambda i: (0, 0)
    wide = pl.BlockSpec((tm, GROUP_W), row)
    f32o = jax.ShapeDtypeStruct((n, GROUP_W), F32)
    bf16o = jax.ShapeDtypeStruct((n, GROUP_W), BF16)
    return pl.pallas_call(
        _in_proj_kernel,
        grid=(nt,),
        in_specs=[pl.BlockSpec((tm, d), row), pl.BlockSpec((1, d), fix),
                  pl.BlockSpec((d, N_GROUPS * GROUP_W), fix),
                  pl.BlockSpec((tm, HEAD_W), pos), pl.BlockSpec((tm, HEAD_W), pos)],
        out_specs=[wide] * 9,
        out_shape=[f32o, f32o, f32o, f32o, bf16o, f32o, f32o, bf16o, bf16o],
        compiler_params=_params(("parallel",)),
        name="in_proj",
    )(x, g, w_bf, cs, sn)


def _retention_kernel(gl_ref, rq_ref, rk_ref, rv_ref, rg_ref, dec_ref, qd_ref, kd_ref,
                      ret_ref, st_ref):
    c = pl.program_id(0)

    @pl.when(c == 0)
    def _():
        st_ref[...] = jnp.zeros(st_ref.shape, F32)

    nb = rq_ref.shape[0]
    for b in range(nb):
        for h in range(RET_HEADS):
            hs = slice(h * HEAD_W, (h + 1) * HEAD_W)
            q = rq_ref[b, :, hs].astype(BF16)
            k = rk_ref[b, :, hs]
            v = rv_ref[b, :, hs].astype(BF16)
            state = st_ref[b, h]
            scores = _dot_nt(q, k.astype(BF16)) * dec_ref[h]
            inner = _dot(scores.astype(BF16), v)
            cross = _dot(q, state.astype(BF16)) * qd_ref[h]
            o = inner + cross
            kd = (k * kd_ref[h]).T.astype(BF16)
            st_ref[b, h] = gl_ref[h] * state + _dot(kd, v)
            rg = rg_ref[b, :, hs]
            o = o * lax.rsqrt(jnp.mean(o * o, axis=-1, keepdims=True) + EPS)
            ret_ref[b, :, hs] = (o * (rg * jax.nn.sigmoid(rg))).astype(BF16)


def _retention_prompt(rq, rk, rv, rg, decay, qdec, kdec, gl):
    b, s, w = rq.shape
    nc = s // RET_CHUNK
    tok = pl.BlockSpec((b, RET_CHUNK, w), lambda c: (0, c, 0))
    const = pl.BlockSpec((RET_HEADS, RET_CHUNK, RET_CHUNK), lambda c: (0, 0, 0))
    return pl.pallas_call(
        _retention_kernel,
        grid=(nc,),
        in_specs=[pl.BlockSpec(memory_space=pltpu.SMEM), tok, tok, tok, tok, const, const, const],
        out_specs=[tok, pl.BlockSpec((b, RET_HEADS, RET_DK, HEAD_W), lambda c: (0, 0, 0, 0))],
        out_shape=[jax.ShapeDtypeStruct((b, s, w), BF16),
                   jax.ShapeDtypeStruct((b, RET_HEADS, RET_DK, HEAD_W), F32)],
        compiler_params=_params(("arbitrary",)),
        name="retention_prompt",
    )(gl, rq, rk, rv, rg, decay, qdec, kdec)


def _retention_step_kernel(gam_ref, rq_ref, rk_ref, rv_ref, rg_ref, st_ref, ret_ref, nst_ref):
    rows = lax.broadcasted_iota(jnp.int32, (RET_DK, HEAD_W), 0)
    for h in range(RET_HEADS):
        hs = slice(h * HEAD_W, (h + 1) * HEAD_W)
        q = rq_ref[0, :, hs]
        k = rk_ref[0, :, hs]
        v = rv_ref[0, :, hs]
        rg = rg_ref[0, :, hs]
        state = st_ref[0, h]
        gam = gam_ref[h]
        q8 = jnp.broadcast_to(q, (8, HEAD_W)).astype(BF16)
        qk = jnp.sum(q.astype(BF16).astype(F32) * k.astype(BF16).astype(F32), axis=-1, keepdims=True)
        cross = _dot(q8, state.astype(BF16))[0:1, :] * gam
        o = qk * v.astype(BF16).astype(F32) + cross
        k_first = jnp.where(rows == 0, jnp.broadcast_to(k, (RET_DK, HEAD_W)), 0.0)
        v_first = jnp.where(rows == 0, jnp.broadcast_to(v, (RET_DK, HEAD_W)), 0.0)
        nst_ref[0, h] = gam * state + _dot(k_first.T.astype(BF16), v_first.astype(BF16))
        o = o * lax.rsqrt(jnp.mean(o * o, axis=-1, keepdims=True) + EPS)
        ret_ref[0, :, hs] = o * (rg * jax.nn.sigmoid(rg))


def _retention_sample(rq, rk, rv, rg, state, gam):
    nb = rq.shape[0]
    tok = pl.BlockSpec((1, 1, GROUP_W), lambda b: (b, 0, 0))
    st = pl.BlockSpec((1, RET_HEADS, RET_DK, HEAD_W), lambda b: (b, 0, 0, 0))
    return pl.pallas_call(
        _retention_step_kernel,
        grid=(nb,),
        in_specs=[pl.BlockSpec(memory_space=pltpu.SMEM), tok, tok, tok, tok, st],
        out_specs=[tok, st],
        out_shape=[jax.ShapeDtypeStruct((nb, 1, GROUP_W), F32),
                   jax.ShapeDtypeStruct(state.shape, F32)],
        compiler_params=_params(("parallel",)),
        name="retention_sample",
    )(gam, rq, rk, rv, rg, state)


def _diff_prompt_kernel(sc_ref, q_ref, k_ref, v_ref, b_ref, o_ref, q2_ref, m_ref, l_ref, acc_ref,
                        *, out_scale):
    h = pl.program_id(1)
    i = pl.program_id(2)
    j = pl.program_id(3)
    t = q_ref.shape[1]

    @pl.when(j == 0)
    def _():
        q = q_ref[0]
        lane = lax.broadcasted_iota(jnp.int32, q.shape, 1)
        zero = jnp.zeros_like(q)
        q2_ref[0:t, :] = jnp.where(lane < DIFF_DH, q, zero)
        q2_ref[t:2 * t, :] = jnp.where(lane < DIFF_DH, zero, q)
        m_ref[...] = jnp.full(m_ref.shape, NEG_INF, F32)
        l_ref[...] = jnp.zeros(l_ref.shape, F32)
        acc_ref[...] = jnp.zeros(acc_ref.shape, F32)

    @pl.when(j <= i)
    def _():
        d = i - j
        s = _dot_nt(q2_ref[...], k_ref[0])
        bias = jnp.where(d == 0, b_ref[0, 0], jnp.where(d == 1, b_ref[0, 1], sc_ref[h]))
        s = s + jnp.concatenate([bias, bias], axis=0)
        m_prev = m_ref[...]
        m_new = jnp.maximum(m_prev, jnp.max(s, axis=-1, keepdims=True))
        alpha = jnp.exp(m_prev - m_new)
        p = jnp.exp(s - m_new)
        l_ref[...] = alpha * l_ref[...] + jnp.sum(p, axis=-1, keepdims=True)
        acc_ref[...] = alpha * acc_ref[...] + _dot(p.astype(BF16), v_ref[0])
        m_ref[...] = m_new

    @pl.when(j == i)
    def _():
        lam = sc_ref[DIFF_HEADS]
        o = acc_ref[...] / l_ref[...]
        o = o[0:t] - lam * o[t:2 * t]
        o = o * lax.rsqrt(jnp.mean(o * o, axis=-1, keepdims=True) + EPS)
        o_ref[0] = (o * out_scale).astype(BF16)


def _diff_prompt(dqb, dkb, dvb, bias_tiles, scalars, out_scale):
    b, s, w = dqb.shape
    t = bias_tiles.shape[-1]
    n = s // t
    qspec = pl.BlockSpec((1, t, HEAD_W), lambda b_, h, i, j: (b_, i, h))
    kspec = pl.BlockSpec((1, t, HEAD_W), lambda b_, h, i, j: (b_, jnp.minimum(j, i), h))
    return pl.pallas_call(
        functools.partial(_diff_prompt_kernel, out_scale=out_scale),
        grid=(b, DIFF_HEADS, n, n),
        in_specs=[pl.BlockSpec(memory_space=pltpu.SMEM), qspec, kspec, kspec,
                  pl.BlockSpec((1, 2, t, t), lambda b_, h, i, j: (h, 0, 0, 0))],
        out_specs=qspec,
        out_shape=jax.ShapeDtypeStruct((b, s, w), BF16),
        scratch_shapes=[pltpu.VMEM((2 * t, HEAD_W), BF16), pltpu.VMEM((2 * t, 1), F32),
                        pltpu.VMEM((2 * t, 1), F32), pltpu.VMEM((2 * t, HEAD_W), F32)],
        compiler_params=_params(("parallel", "parallel", "parallel", "arbitrary")),
        name="diff_attn_prompt",
    )(scalars, dqb, dkb, dvb, bias_tiles)


def _diff_sample_kernel(pt_ref, sc_ref, q_ref, kn_ref, vn_ref, bias_ref, *rest, n_pp, out_scale):
    k_refs = rest[:n_pp]
    v_refs = rest[n_pp:2 * n_pp]
    o_ref, qm_ref, m_ref, l_ref, acc_ref = rest[2 * n_pp:]
    j = pl.program_id(1)
    n_rows = 2 * DIFF_HEADS

    @pl.when(j == 0)
    def _():
        q = jnp.broadcast_to(q_ref[0], (n_rows, GROUP_W))
        lane = lax.broadcasted_iota(jnp.int32, (n_rows, GROUP_W), 1)
        row = lax.broadcasted_iota(jnp.int32, (n_rows, GROUP_W), 0)
        qm = jnp.where(lax.shift_right_logical(lane, int(math.log2(DIFF_DH))) == row, q, 0.0)
        qm_ref[...] = qm
        kn = kn_ref[0].astype(BF16).astype(F32)
        s_new = jnp.sum(qm * kn, axis=-1, keepdims=True)
        head = lax.shift_right_logical(lax.broadcasted_iota(jnp.int32, (n_rows, 1), 0), 1)
        b0 = jnp.zeros((n_rows, 1), F32)
        for hh in range(DIFF_HEADS):
            b0 = jnp.where(head == hh, sc_ref[DIFF_HEADS + 1 + hh], b0)
        m_ref[...] = s_new + b0
        l_ref[...] = jnp.ones(l_ref.shape, F32)
        acc_ref[...] = jnp.broadcast_to(vn_ref[0].astype(BF16).astype(F32), (n_rows, GROUP_W))

    qm = qm_ref[...].astype(BF16)
    s = jnp.concatenate([_dot_nt(qm, k_refs[u][0].astype(BF16)) for u in range(n_pp)], axis=1)
    s = s + bias_ref[...]
    m_prev = m_ref[...]
    m_new = jnp.maximum(m_prev, jnp.max(s, axis=-1, keepdims=True))
    alpha = jnp.exp(m_prev - m_new)
    p = jnp.exp(s - m_new)
    l_ref[...] = alpha * l_ref[...] + jnp.sum(p, axis=-1, keepdims=True)
    pv = _dot(p[:, 0:PAGE].astype(BF16), v_refs[0][0].astype(BF16))
    for u in range(1, n_pp):
        pv = pv + _dot(p[:, u * PAGE:(u + 1) * PAGE].astype(BF16), v_refs[u][0].astype(BF16))
    acc_ref[...] = alpha * acc_ref[...] + pv
    m_ref[...] = m_new

    @pl.when(j == pl.num_programs(1) - 1)
    def _():
        lam = sc_ref[DIFF_HEADS]
        o = acc_ref[...] / l_ref[...]
        for hh in range(DIFF_HEADS):
            hs = slice(hh * HEAD_W, (hh + 1) * HEAD_W)
            oh = o[2 * hh:2 * hh + 1, hs] - lam * o[2 * hh + 1:2 * hh + 2, hs]
            oh = oh * lax.rsqrt(jnp.mean(oh * oh, axis=-1, keepdims=True) + EPS)
            o_ref[0, :, hs] = oh * out_scale


def _diff_sample(dqb, dk_new, dv_new, cache_k, cache_v, page_table, bias_past, scalars, out_scale):
    nb = dqb.shape[0]
    n_pages = page_table.shape[1]
    n_pp = PAGES_PER_STEP
    while n_pages % n_pp:
        n_pp //= 2
    tok = pl.BlockSpec((1, 1, GROUP_W), lambda b, j, pt: (b, 0, 0))

    def page_spec(u):
        return pl.BlockSpec((1, PAGE, GROUP_W), lambda b, j, pt: (pt[b, j * n_pp + u], 0, 0))

    n_rows = 2 * DIFF_HEADS
    grid_spec = pltpu.PrefetchScalarGridSpec(
        num_scalar_prefetch=1,
        grid=(nb, n_pages // n_pp),
        in_specs=[pl.BlockSpec(memory_space=pltpu.SMEM), tok, tok, tok,
                  pl.BlockSpec((n_rows, n_pp * PAGE), lambda b, j, pt: (0, j))]
                 + [page_spec(u) for u in range(n_pp)] * 2,
        out_specs=tok,
        scratch_shapes=[pltpu.VMEM((n_rows, GROUP_W), F32), pltpu.VMEM((n_rows, 1), F32),
                        pltpu.VMEM((n_rows, 1), F32), pltpu.VMEM((n_rows, GROUP_W), F32)],
    )
    return pl.pallas_call(
        functools.partial(_diff_sample_kernel, n_pp=n_pp, out_scale=out_scale),
        grid_spec=grid_spec,
        out_shape=jax.ShapeDtypeStruct((nb, 1, GROUP_W), F32),
        compiler_params=_params(("parallel", "arbitrary")),
        name="diff_attn_sample",
    )(page_table, scalars, dqb, dk_new, dv_new, bias_past,
      *([cache_k] * n_pp), *([cache_v] * n_pp))


def _mem_kv_kernel(m_ref, g_ref, wk_ref, wv_ref, k_ref, v_ref, kb_ref, vb_ref):
    mn = _rms(m_ref[...], g_ref[...]).astype(BF16)
    k = _dot(mn, wk_ref[...])
    v = _dot(mn, wv_ref[...])
    k_ref[...] = k
    v_ref[...] = v
    kb_ref[...] = k.astype(BF16)
    vb_ref[...] = v.astype(BF16)


def _mem_kv(mem, g, wk_bf, wv_bf, tm):
    n, d = mem.shape
    row = pl.BlockSpec((tm, d), lambda i: (i, 0))
    fix = lambda i: (0, 0)
    return pl.pallas_call(
        _mem_kv_kernel,
        grid=(n // tm,),
        in_specs=[row, pl.BlockSpec((1, d), fix), pl.BlockSpec((d, d), fix), pl.BlockSpec((d, d), fix)],
        out_specs=[row] * 4,
        out_shape=[jax.ShapeDtypeStruct((n, d), F32)] * 2 + [jax.ShapeDtypeStruct((n, d), BF16)] * 2,
        compiler_params=_params(("parallel",)),
        name="mem_kv",
    )(mem, g, wk_bf, wv_bf)


def _mix_residual(x, ret, dif, w_out_ref):
    half = w_out_ref.shape[0] // 2
    return x + _dot(ret.astype(BF16), w_out_ref[0:half, :]) + _dot(dif.astype(BF16), w_out_ref[half:, :])


def _cross_heads(q, mk, mv, o_ref):
    dh = q.shape[1] // MEM_HEADS
    for h in range(MEM_HEADS):
        hs = slice(h * dh, (h + 1) * dh)
        s = _dot_nt(q[:, hs], mk[:, hs]) * (dh ** -0.5)
        e = jnp.exp(s - jnp.max(s, axis=-1, keepdims=True))
        p = e / jnp.sum(e, axis=-1, keepdims=True)
        o_ref[:, hs] = _dot(p.astype(BF16), mv[:, hs]).astype(o_ref.dtype)


def _post_mix_kernel(x_ref, ret_ref, dif_ref, mk_ref, mv_ref, wo_ref, gc_ref, wq_ref, wm_ref, gf_ref,
                     h_ref, xt_ref, o_scr):
    h1 = _mix_residual(x_ref[...], ret_ref[...], dif_ref[...], wo_ref)
    q = _dot(_rms(h1, gc_ref[...]).astype(BF16), wq_ref[...]).astype(BF16)
    _cross_heads(q, mk_ref[0], mv_ref[0], o_scr)
    h2 = h1 + _dot(o_scr[...], wm_ref[...])
    h_ref[...] = h2
    xt_ref[...] = _rms(h2, gf_ref[...]).T.astype(BF16)


def _post_mix(x, ret, dif, mk_bf, mv_bf, wo_bf, g_cross, wq_bf, wm_bf, g_ffn, tm):
    n, d = x.shape
    nb, m, _ = mk_bf.shape
    per_batch = n // nb // tm
    row = lambda i: (i, 0)
    fix = lambda i: (0, 0)
    mem = pl.BlockSpec((1, m, d), lambda i: (i // per_batch, 0, 0))
    half = pl.BlockSpec((tm, GROUP_W), row)
    return pl.pallas_call(
        _post_mix_kernel,
        grid=(n // tm,),
        in_specs=[pl.BlockSpec((tm, d), row), half, half, mem, mem,
                  pl.BlockSpec((d, d), fix), pl.BlockSpec((1, d), fix), pl.BlockSpec((d, d), fix),
                  pl.BlockSpec((d, d), fix), pl.BlockSpec((1, d), fix)],
        out_specs=[pl.BlockSpec((tm, d), row), pl.BlockSpec((d, tm), lambda i: (0, i))],
        out_shape=[jax.ShapeDtypeStruct((n, d), F32), jax.ShapeDtypeStruct((d, n), BF16)],
        scratch_shapes=[pltpu.VMEM((tm, d), BF16)],
        compiler_params=_params(("parallel",)),
        name="post_mix",
    )(x, ret, dif, mk_bf, mv_bf, wo_bf, g_cross, wq_bf, wm_bf, g_ffn)


def _sample_mix_kernel(x_ref, ret_ref, dif_ref, wo_ref, gc_ref, wq_ref, h_ref, q_ref):
    h1 = _mix_residual(x_ref[...], ret_ref[...], dif_ref[...], wo_ref)
    h_ref[...] = h1
    q_ref[...] = _dot(_rms(h1, gc_ref[...]).astype(BF16), wq_ref[...])


def _sample_cross_kernel(q_ref, mk_ref, mv_ref, o_ref, o_scr):
    q8 = jnp.broadcast_to(q_ref[0], (8, q_ref.shape[2])).astype(BF16)
    _cross_heads(q8, mk_ref[0].astype(BF16), mv_ref[0].astype(BF16), o_scr)
    o_ref[0] = o_scr[0:1, :]


def _sample_out_kernel(h_ref, o_ref, wm_ref, gf_ref, h2_ref, xt_ref, *, lanes):
    h2 = h_ref[...] + _dot(o_ref[...].astype(BF16), wm_ref[...])
    h2_ref[...] = h2
    xb = _rms(h2, gf_ref[...])
    pad = jnp.zeros((lanes - xb.shape[0], xb.shape[1]), F32)
    xt_ref[...] = jnp.concatenate([xb, pad], axis=0).T.astype(BF16)


def _post_mix_sample(x, ret, dif, mem_k, mem_v, wo_bf, g_cross, wq_bf, wm_bf, g_ffn, lanes):
    n, d = x.shape
    m = mem_k.shape[1]
    whole = lambda shape: pl.BlockSpec(shape, lambda: tuple(0 for _ in shape))
    h1, q = pl.pallas_call(
        _sample_mix_kernel,
        in_specs=[whole((n, d)), whole((n, GROUP_W)), whole((n, GROUP_W)), whole((d, d)),
                  whole((1, d)), whole((d, d))],
        out_specs=[whole((n, d)), whole((n, d))],
        out_shape=[jax.ShapeDtypeStruct((n, d), F32), jax.ShapeDtypeStruct((n, d), F32)],
        compiler_params=_params(()),
        name="sample_mix",
    )(x, ret, dif, wo_bf, g_cross, wq_bf)
    tok = pl.BlockSpec((1, 1, d), lambda b: (b, 0, 0))
    mem = pl.BlockSpec((1, m, d), lambda b: (b, 0, 0))
    o = pl.pallas_call(
        _sample_cross_kernel,
        grid=(n,),
        in_specs=[tok, mem, mem],
        out_specs=tok,
        out_shape=jax.ShapeDtypeStruct((n, 1, d), F32),
        scratch_shapes=[pltpu.VMEM((8, d), F32)],
        compiler_params=_params(("parallel",)),
        name="sample_cross",
    )(q.reshape(n, 1, d), mem_k, mem_v)
    return pl.pallas_call(
        functools.partial(_sample_out_kernel, lanes=lanes),
        in_specs=[whole((n, d)), whole((n, d)), whole((d, d)), whole((1, d))],
        out_specs=[whole((n, d)), whole((d, lanes))],
        out_shape=[jax.ShapeDtypeStruct((n, d), F32), jax.ShapeDtypeStruct((d, lanes), BF16)],
        compiler_params=_params(()),
        name="sample_out",
    )(h1, o.reshape(n, d), wm_bf, g_ffn)


def _candidate_tiles():
    tiles = [("j", 0, 0, ()), ("j", 0, 8, ()), ("j", 1, 0, ())]
    tiles += [("i", 0, 0, (0, 1)), ("i", 0, 8, ())]
    tiles += [("i", j, 0, (0, 1)) for j in (1, 2, 3, 4)]
    return tiles


def _top_k_rows(s, k):
    n, t = s.shape
    rows = lax.broadcasted_iota(jnp.int32, s.shape, 0).astype(F32)
    sub = lax.broadcasted_iota(jnp.int32, (8, t), 0)
    rank = jnp.full(s.shape, float(k), F32)
    blocks = [jnp.zeros((8, t), F32) for _ in range(k // 8)]
    for it in range(k):
        m = jnp.max(s, axis=0, keepdims=True)
        first = jnp.min(jnp.where(s == m, rows, float(n)), axis=0, keepdims=True)
        sel = rows == first
        rank = jnp.where(sel, float(it), rank)
        s = jnp.where(sel, -jnp.inf, s)
        blocks[it // 8] = jnp.where(sub == it % 8, m, blocks[it // 8])
    return blocks, rank


def _route_kernel(xt_ref, wq_ref, keys_ref, rank2_ref, e2w_ref, cnt1_ref, e1w_ref, q_scr):
    q_scr[...] = _dot(wq_ref[...], xt_ref[...]).astype(BF16)
    t = xt_ref.shape[1]
    k = PEER_TOPK
    tiles = _candidate_tiles()
    sub = lax.broadcasted_iota(jnp.int32, (8, t), 0)

    def head(h, carry):
        base = pl.multiple_of(h * PEER_DQ, PEER_DQ)
        s1 = _dot(keys_ref[h, 0], q_scr[pl.ds(base, PEER_DQ // 2), :])
        s2 = _dot(keys_ref[h, 1], q_scr[pl.ds(base + PEER_DQ // 2, PEER_DQ // 2), :])
        a, rank1 = _top_k_rows(s1, k)
        b, rank2 = _top_k_rows(s2, k)

        def row_of(blocks, i):
            return blocks[i // 8][i % 8:i % 8 + 1, :]

        cands, poss = [], []
        for kind, fixed, start, skip in tiles:
            if kind == "j":
                c = row_of(a, fixed) + b[start // 8]
                pos = fixed * k + start + sub
            else:
                c = a[start // 8] + row_of(b, fixed)
                pos = (start + sub) * k + fixed
                for r in skip:
                    c = jnp.where(sub == r, -jnp.inf, c)
                    pos = jnp.where(sub == r, k * k, pos)
            cands.append(c)
            poss.append(pos.astype(F32))
        c_all = jnp.concatenate(cands, axis=0)
        pos_all = jnp.concatenate(poss, axis=0)
        top = c_all[0:1, :]
        picked = jnp.zeros(c_all.shape, F32)
        work = c_all
        for _ in range(k):
            m = jnp.max(work, axis=0, keepdims=True)
            first = jnp.min(jnp.where(work == m, pos_all, float(k * k)), axis=0, keepdims=True)
            sel = pos_all == first
            picked = jnp.where(sel, 1.0, picked)
            work = jnp.where(sel, -jnp.inf, work)
        z = jnp.sum(picked * jnp.exp(jnp.where(picked > 0, c_all, top) - top), axis=0, keepdims=True)

        cnt_lo = jnp.zeros((8, t), F32)
        cnt_hi = jnp.zeros((8, t), F32)
        for n_tile, (kind, fixed, start, skip) in enumerate(tiles):
            pk = picked[n_tile * 8:(n_tile + 1) * 8, :]
            if kind == "j":
                tot = jnp.sum(pk, axis=0, keepdims=True)
                add = jnp.where(sub == fixed % 8, tot, 0.0)
                if fixed < 8:
                    cnt_lo = cnt_lo + add
                else:
                    cnt_hi = cnt_hi + add
            elif start == 0:
                cnt_lo = cnt_lo + pk
            else:
                cnt_hi = cnt_hi + pk
        cnt = [cnt_lo, cnt_hi]

        cnt1 = jnp.zeros(s1.shape, F32)
        for i in range(k):
            cnt1 = jnp.where(rank1 == float(i), row_of(cnt, i), cnt1)
        rank2_ref[h] = rank2
        e2w_ref[h] = jnp.exp(s2 - row_of(b, 0))
        cnt1_ref[h] = cnt1
        e1w_ref[h] = jnp.exp(s1 - row_of(a, 0)) / z
        return carry

    lax.fori_loop(0, PEER_HEADS, head, 0)


def _route(xt, wq_t, keys_bf):
    d, n = xt.shape
    t = ROUTE_TILE
    table = pl.BlockSpec((PEER_HEADS, N_KEYS, t), lambda i: (0, 0, i))
    shape = jax.ShapeDtypeStruct((PEER_HEADS, N_KEYS, n), F32)
    return pl.pallas_call(
        _route_kernel,
        grid=(n // t,),
        in_specs=[pl.BlockSpec((d, t), lambda i: (0, i)),
                  pl.BlockSpec(wq_t.shape, lambda i: (0, 0)),
                  pl.BlockSpec(keys_bf.shape, lambda i: (0, 0, 0, 0))],
        out_specs=[table] * 4,
        out_shape=[shape] * 4,
        scratch_shapes=[pltpu.VMEM((PEER_HEADS * PEER_DQ, t), BF16)],
        compiler_params=_params(("parallel",)),
        name="peer_route",
    )(xt, wq_t, keys_bf)


def _peer_kernel(xt_ref, u_ref, vt_ref, rank2_ref, e2w_ref, cnt1_ref, e1w_ref, h_ref, g_ref,
                 y_ref, acc_ref, a_ref, gate_ref):
    j = pl.program_id(1)
    te = u_ref.shape[0]
    tm = xt_ref.shape[1]
    rows_per_step = te // N_KEYS
    lane_w = 128

    @pl.when(j == 0)
    def _():
        acc_ref[...] = jnp.zeros(acc_ref.shape, F32)

    a_ref[...] = _dot(u_ref[...], xt_ref[...])

    e1_base = pl.multiple_of(j * rows_per_step, 8)
    for r in range(rows_per_step):
        rs = slice(r * N_KEYS, (r + 1) * N_KEYS)
        for c in range(tm // lane_w):
            ls = slice(c * lane_w, (c + 1) * lane_w)
            gate = jnp.zeros((N_KEYS, lane_w), F32)
            for h in range(PEER_HEADS):
                cnt = cnt1_ref[h, pl.ds(e1_base + 8 * (r // 8), 8), ls][r % 8:r % 8 + 1, :]
                w1 = e1w_ref[h, pl.ds(e1_base + 8 * (r // 8), 8), ls][r % 8:r % 8 + 1, :]
                gate = gate + jnp.where(rank2_ref[h, :, ls] < cnt, e2w_ref[h, :, ls], 0.0) * w1
            a = a_ref[rs, ls]
            act = 0.5 * a * (1.0 + lax.erf(a * (2.0 ** -0.5)))
            gate_ref[rs, ls] = (act * gate).astype(BF16)

    acc_ref[...] += _dot(vt_ref[...], gate_ref[...])

    @pl.when(j == pl.num_programs(1) - 1)
    def _():
        y_ref[...] = _rms(h_ref[...] + acc_ref[...].T, g_ref[...])


def _peer(xt, u_bf, vt_bf, tables, h2, g_final, tm):
    d, n = xt.shape
    ne = u_bf.shape[0]
    te = PEER_EXPERT_TILE
    table = pl.BlockSpec((PEER_HEADS, N_KEYS, tm), lambda i, j: (0, 0, i))
    return pl.pallas_call(
        _peer_kernel,
        grid=(n // tm, ne // te),
        in_specs=[pl.BlockSpec((d, tm), lambda i, j: (0, i)),
                  pl.BlockSpec((te, d), lambda i, j: (j, 0)),
                  pl.BlockSpec((d, te), lambda i, j: (0, j)),
                  table, table, table, table,
                  pl.BlockSpec((tm, d), lambda i, j: (i, 0)),
                  pl.BlockSpec((1, d), lambda i, j: (0, 0))],
        out_specs=pl.BlockSpec((tm, d), lambda i, j: (i, 0)),
        out_shape=jax.ShapeDtypeStruct((n, d), F32),
        scratch_shapes=[pltpu.VMEM((d, tm), F32), pltpu.VMEM((te, tm), F32), pltpu.VMEM((te, tm), BF16)],
        compiler_params=_params(("parallel", "arbitrary")),
        name="peer_dense",
    )(xt, u_bf, vt_bf, *tables, h2, g_final)


def _t5_bias(rel, table):
    n = jnp.maximum(rel, 0)
    max_exact = N_BUCKETS // 2
    nf = jnp.maximum(n, 1).astype(F32)
    large = max_exact + (jnp.log(nf / max_exact) / math.log(MAX_DISTANCE / max_exact)
                         * (N_BUCKETS - max_exact)).astype(jnp.int32)
    large = jnp.minimum(large, N_BUCKETS - 1)
    bucket = jnp.where(n < max_exact, n, large)
    return jnp.moveaxis(table[bucket].astype(F32), -1, 0)


def _toeplitz(vec, t):
    hh = vec.shape[0]
    flat = jnp.tile(vec, (1, t))[:, :t * (2 * t - 1)]
    return flat.reshape(hh, t, 2 * t - 1)[:, :, :t]


def _prompt_bias_tiles(rel_bias, t):
    k = jnp.arange(2 * t)
    tiles = []
    for d in (0, 1):
        rel = jnp.where(k < t, d * t - k, d * t + 2 * t - k)
        vec = _t5_bias(rel, rel_bias)
        vec = jnp.where(rel[None, :] >= 0, vec, NEG_INF)
        tiles.append(_toeplitz(vec, t))
    return jnp.stack(tiles, axis=1)


def _rotary_tables(pos):
    half = RET_DK // 2
    inv = 1.0 / (ROPE_BASE ** (jnp.arange(half, dtype=F32) / half))
    ang = pos.astype(F32)[:, None] * inv[None, :]
    cos, sin = jnp.cos(ang), jnp.sin(ang)
    return jnp.concatenate([cos, cos], axis=1), jnp.concatenate([-sin, sin], axis=1)


def _retention_tables(log_gamma):
    n = jnp.arange(RET_CHUNK, dtype=F32)
    diff = n[:, None] - n[None, :]
    decay = jnp.where(diff[None] >= 0,
                      jnp.exp(jnp.maximum(diff, 0.0)[None] * log_gamma[:, None, None]), 0.0)
    ones = jnp.ones((1, 1, HEAD_W), F32)
    qdec = jnp.exp((n + 1.0)[None, :] * log_gamma[:, None])[:, :, None] * ones
    kdec = jnp.exp((RET_CHUNK - 1.0 - n)[None, :] * log_gamma[:, None])[:, :, None] * ones
    return decay, qdec, kdec, jnp.exp(RET_CHUNK * log_gamma)


def kernel(x_prompt, x_sample, mem_prompt, cache_diff_k, cache_diff_v, page_table, state_ret,
           cache_mem_k, cache_mem_v, g_mix, w_in, w_out, lam_q1, lam_k1, lam_q2, lam_k2, rel_bias,
           g_cross, g_mem, w_mq, w_mk, w_mv, w_mo, g_ffn, w_pq, sub_keys, peer_u, peer_v, g_final):
    depth = g_mix.shape[0]
    assert depth == 1
    l = 0
    b, s, d = x_prompt.shape
    db, t_s, _ = x_sample.shape
    assert t_s == 1
    n_pages = page_table.shape[1]
    past = n_pages * PAGE
    mem_len = mem_prompt.shape[1]
    n_tok = b * s

    log_gamma = jnp.log(1.0 - 2.0 ** (-5.0 - jnp.arange(RET_HEADS, dtype=F32)))
    lam_init = 0.8 - 0.6 * math.exp(-0.3 * l)
    lam = (jnp.exp(jnp.sum(lam_q1[l].astype(F32) * lam_k1[l].astype(F32)))
           - jnp.exp(jnp.sum(lam_q2[l].astype(F32) * lam_k2[l].astype(F32))) + lam_init)
    out_scale = 1.0 - lam_init

    row = lambda v: v.reshape(1, -1)
    w_in_bf = w_in[l].astype(BF16)
    w_out_bf = w_out[l].astype(BF16)
    w_mq_bf = w_mq[l].astype(BF16)
    w_mk_bf = w_mk[l].astype(BF16)
    w_mv_bf = w_mv[l].astype(BF16)
    w_mo_bf = w_mo[l].astype(BF16)
    w_pq_t = w_pq[l].T.astype(BF16)
    keys_bf = sub_keys[l].astype(BF16)
    u_bf = peer_u[l].astype(BF16)
    vt_bf = peer_v[l].T.astype(BF16)
    g_fin = row(g_final)

    bias_far = _t5_bias(jnp.full((1,), MAX_DISTANCE, jnp.int32), rel_bias)[:, 0]
    bias_zero = _t5_bias(jnp.zeros((1,), jnp.int32), rel_bias)[:, 0]
    scalars = jnp.concatenate([bias_far, lam.reshape(1), bias_zero]).astype(F32)

    cs_p, sn_p = _rotary_tables(jnp.arange(s))
    rq, rk, rv, rg, dqb, dk, dv, dkb, dvb = _in_proj(
        x_prompt.reshape(n_tok, d), row(g_mix[l]), w_in_bf, cs_p, sn_p, TOKEN_TILE)
    decay, qdec, kdec, gl = _retention_tables(log_gamma)
    shp = (b, s, GROUP_W)
    ret_p, state_p = _retention_prompt(rq.reshape(shp), rk.reshape(shp), rv.reshape(shp),
                                       rg.reshape(shp), decay, qdec, kdec, gl)
    dif_p = _diff_prompt(dqb.reshape(shp), dkb.reshape(shp), dvb.reshape(shp),
                         _prompt_bias_tiles(rel_bias, ATTN_TILE), scalars, out_scale)
    mk, mv, mk_bf, mv_bf = _mem_kv(mem_prompt.reshape(b * mem_len, d), row(g_mem[l]),
                                   w_mk_bf, w_mv_bf, mem_len)
    h2_p, xt_p = _post_mix(x_prompt.reshape(n_tok, d), ret_p.reshape(n_tok, GROUP_W),
                           dif_p.reshape(n_tok, GROUP_W), mk_bf.reshape(b, mem_len, d),
                           mv_bf.reshape(b, mem_len, d), w_out_bf, row(g_cross[l]), w_mq_bf,
                           w_mo_bf, row(g_ffn[l]), TOKEN_TILE)
    tables_p = _route(xt_p, w_pq_t, keys_bf)
    y_p = _peer(xt_p, u_bf, vt_bf, tables_p, h2_p, g_fin, TOKEN_TILE)

    lanes = ROUTE_TILE
    assert db <= lanes
    cs_s, sn_s = _rotary_tables(jnp.full((db,), past, jnp.int32))
    xs = x_sample.reshape(db, d)
    rq_s, rk_s, rv_s, rg_s, dqb_s, dk_s, dv_s, _, _ = _in_proj(
        xs, row(g_mix[l]), w_in_bf, cs_s, sn_s, db)
    tok = (db, 1, GROUP_W)
    ret_s, state_s = _retention_sample(rq_s.reshape(tok), rk_s.reshape(tok), rv_s.reshape(tok),
                                       rg_s.reshape(tok), state_ret[l], jnp.exp(log_gamma))
    rel_past = past - jnp.arange(past)
    bias_past = jnp.repeat(_t5_bias(rel_past, rel_bias), 2, axis=0)
    dif_s = _diff_sample(dqb_s.astype(F32).reshape(tok), dk_s.reshape(tok), dv_s.reshape(tok),
                         cache_diff_k[l].reshape(-1, PAGE, GROUP_W),
                         cache_diff_v[l].reshape(-1, PAGE, GROUP_W),
                         page_table, bias_past, scalars, out_scale)
    h2_s, xt_s = _post_mix_sample(xs, ret_s.reshape(db, GROUP_W), dif_s.reshape(db, GROUP_W),
                                  cache_mem_k[l].reshape(db, mem_len, d),
                                  cache_mem_v[l].reshape(db, mem_len, d),
                                  w_out_bf, row(g_cross[l]), w_mq_bf, w_mo_bf, row(g_ffn[l]), lanes)
    tables_s = _route(xt_s, w_pq_t, keys_bf)
    h2_pad = jnp.concatenate([h2_s, jnp.zeros((lanes - db, d), F32)], axis=0)
    y_s = _peer(xt_s, u_bf, vt_bf, tables_s, h2_pad, g_fin, lanes)[:db]

    return (y_p.reshape(b, s, d), y_s.reshape(db, 1, d),
            dk.reshape(1, b, s, DIFF_HEADS, 2, DIFF_DH), dv.reshape(1, b, s, DIFF_HEADS, HEAD_W),
            state_p[None], mk.reshape(1, b, mem_len, MEM_HEADS, d // MEM_HEADS),
            mv.reshape(1, b, mem_len, MEM_HEADS, d // MEM_HEADS),
            dk_s.reshape(1, db, 1, DIFF_HEADS, 2, DIFF_DH), dv_s.reshape(1, db, 1, DIFF_HEADS, HEAD_W),
            state_s[None])
```

```python
import functools
import math

import jax
import jax.numpy as jnp
import numpy as np
from jax import lax
from jax.experimental import pallas as pl
from jax.experimental.pallas import tpu as pltpu

F32 = jnp.float32
BF16 = jnp.bfloat16

EPS = 1e-6
NEG_INF = -1e30
ROPE_BASE = 10000.0
RET_HEADS = 4
RET_DK = 128
RET_CHUNK = 128
DIFF_HEADS = 4
DIFF_DH = 64
HEAD_W = 128
GROUP_W = 512
N_GROUPS = 7
PAGE = 128
N_BUCKETS = 32
MAX_DISTANCE = 128
MEM_HEADS = 4
PEER_HEADS = 8
N_KEYS = 128
PEER_TOPK = 16
PEER_DQ = 256

VMEM_LIMIT = 56 * 1024 * 1024
ATTN_TILE = 512
ATTN_ROW_CHUNKS = 4
TOKEN_TILE = 512
PEER_EXPERT_TILE = 1024
PEER_VMEM_LIMIT = 60 * 1024 * 1024
ROUTE_TILE = 256
SAMPLE_LANES = 128
PAGES_PER_STEP = 8


def _params(semantics, vmem=VMEM_LIMIT):
    return pltpu.CompilerParams(dimension_semantics=semantics, vmem_limit_bytes=vmem)


def _rms(x, g):
    return (x * lax.rsqrt(jnp.mean(x * x, axis=-1, keepdims=True) + EPS)) * g


def _dot(a, b):
    return jnp.dot(a, b, preferred_element_type=F32)


def _dot_nt(a, b):
    return lax.dot_general(a, b, (((1,), (1,)), ((), ())), preferred_element_type=F32)


def _in_proj_kernel(x_ref, g_ref, w_ref, cs_ref, sn_ref,
                    rq_ref, rk_ref, rv_ref, rg_ref, dqb_ref, dk_ref, dv_ref, dkb_ref, dvb_ref):
    xb = _rms(x_ref[...], g_ref[...]).astype(BF16)
    cs = cs_ref[...]
    sn = sn_ref[...]

    def proj(c):
        return _dot(xb, w_ref[:, c * GROUP_W:(c + 1) * GROUP_W])

    def rotary(p, out_ref, scale):
        for h in range(RET_HEADS):
            ph = p[:, h * HEAD_W:(h + 1) * HEAD_W]
            r = ph * cs + pltpu.roll(ph, HEAD_W // 2, 1) * sn
            if scale is not None:
                r = r * scale
            out_ref[:, h * HEAD_W:(h + 1) * HEAD_W] = r

    rotary(proj(0), rq_ref, RET_DK ** -0.5)
    rotary(proj(1), rk_ref, None)
    rv_ref[...] = proj(2)
    rg_ref[...] = proj(3)
    dqb_ref[...] = (proj(4) * (DIFF_DH ** -0.5)).astype(BF16)
    dk = proj(5)
    dk_ref[...] = dk
    dkb_ref[...] = dk.astype(BF16)
    dv = proj(6)
    dv_ref[...] = dv
    dvb_ref[...] = dv.astype(BF16)


def _in_proj(x, g, w_bf, cs, sn, tm):
    n, d = x.shape
    nt = n // tm
    n_pos = cs.shape[0] // tm
    row = lambda i: (i, 0)
    pos = lambda i: (i % n_pos, 0)
    fix = lambda i: (0, 0)
    wide = pl.BlockSpec((tm, GROUP_W), row)
    f32o = jax.ShapeDtypeStruct((n, GROUP_W), F32)
    bf16o = jax.ShapeDtypeStruct((n, GROUP_W), BF16)
    return pl.pallas_call(
        _in_proj_kernel,
        grid=(nt,),
        in_specs=[pl.BlockSpec((tm, d), row), pl.BlockSpec((1, d), fix),
                  pl.BlockSpec((d, N_GROUPS * GROUP_W), fix),
                  pl.BlockSpec((tm, HEAD_W), pos), pl.BlockSpec((tm, HEAD_W), pos)],
        out_specs=[wide] * 9,
        out_shape=[f32o, f32o, f32o, f32o, bf16o, f32o, f32o, bf16o, bf16o],
        compiler_params=_params(("parallel",)),
        name="in_proj",
    )(x, g, w_bf, cs, sn)


def _retention_kernel(gl_ref, rq_ref, rk_ref, rv_ref, rg_ref, dec_ref, qd_ref, kd_ref,
                      ret_ref, st_ref):
    c = pl.program_id(0)

    @pl.when(c == 0)
    def _():
        st_ref[...] = jnp.zeros(st_ref.shape, F32)

    nb = rq_ref.shape[0]
    for b in range(nb):
        for h in range(RET_HEADS):
            hs = slice(h * HEAD_W, (h + 1) * HEAD_W)
            q = rq_ref[b, :, hs].astype(BF16)
            k = rk_ref[b, :, hs]
            v = rv_ref[b, :, hs].astype(BF16)
            state = st_ref[b, h]
            scores = _dot_nt(q, k.astype(BF16)) * dec_ref[h]
            inner = _dot(scores.astype(BF16), v)
            cross = _dot(q, state.astype(BF16)) * qd_ref[h]
            o = inner + cross
            kd = (k * kd_ref[h]).T.astype(BF16)
            st_ref[b, h] = gl_ref[h] * state + _dot(kd, v)
            rg = rg_ref[b, :, hs]
            o = o * lax.rsqrt(jnp.mean(o * o, axis=-1, keepdims=True) + EPS)
            ret_ref[b, :, hs] = (o * (rg * jax.nn.sigmoid(rg))).astype(BF16)


def _retention_prompt(rq, rk, rv, rg, decay, qdec, kdec, gl):
    b, s, w = rq.shape
    nc = s // RET_CHUNK
    tok = pl.BlockSpec((b, RET_CHUNK, w), lambda c: (0, c, 0))
    const = pl.BlockSpec((RET_HEADS, RET_CHUNK, RET_CHUNK), lambda c: (0, 0, 0))
    return pl.pallas_call(
        _retention_kernel,
        grid=(nc,),
        in_specs=[pl.BlockSpec(memory_space=pltpu.SMEM), tok, tok, tok, tok, const, const, const],
        out_specs=[tok, pl.BlockSpec((b, RET_HEADS, RET_DK, HEAD_W), lambda c: (0, 0, 0, 0))],
        out_shape=[jax.ShapeDtypeStruct((b, s, w), BF16),
                   jax.ShapeDtypeStruct((b, RET_HEADS, RET_DK, HEAD_W), F32)],
        compiler_params=_params(("arbitrary",)),
        name="retention_prompt",
    )(gl, rq, rk, rv, rg, decay, qdec, kdec)


def _retention_step_kernel(gam_ref, rq_ref, rk_ref, rv_ref, rg_ref, st_ref, ret_ref, nst_ref):
    rows = lax.broadcasted_iota(jnp.int32, (RET_DK, HEAD_W), 0)
    for h in range(RET_HEADS):
        hs = slice(h * HEAD_W, (h + 1) * HEAD_W)
        q = rq_ref[0, :, hs]
        k = rk_ref[0, :, hs]
        v = rv_ref[0, :, hs]
        rg = rg_ref[0, :, hs]
        state = st_ref[0, h]
        gam = gam_ref[h]
        q8 = jnp.broadcast_to(q, (8, HEAD_W)).astype(BF16)
        qk = jnp.sum(q.astype(BF16).astype(F32) * k.astype(BF16).astype(F32), axis=-1, keepdims=True)
        cross = _dot(q8, state.astype(BF16))[0:1, :] * gam
        o = qk * v.astype(BF16).astype(F32) + cross
        k_first = jnp.where(rows == 0, jnp.broadcast_to(k, (RET_DK, HEAD_W)), 0.0)
        v_first = jnp.where(rows == 0, jnp.broadcast_to(v, (RET_DK, HEAD_W)), 0.0)
        nst_ref[0, h] = gam * state + _dot(k_first.T.astype(BF16), v_first.astype(BF16))
        o = o * lax.rsqrt(jnp.mean(o * o, axis=-1, keepdims=True) + EPS)
        ret_ref[0, :, hs] = o * (rg * jax.nn.sigmoid(rg))


def _retention_sample(rq, rk, rv, rg, state, gam):
    nb = rq.shape[0]
    tok = pl.BlockSpec((1, 1, GROUP_W), lambda b: (b, 0, 0))
    st = pl.BlockSpec((1, RET_HEADS, RET_DK, HEAD_W), lambda b: (b, 0, 0, 0))
    return pl.pallas_call(
        _retention_step_kernel,
        grid=(nb,),
        in_specs=[pl.BlockSpec(memory_space=pltpu.SMEM), tok, tok, tok, tok, st],
        out_specs=[tok, st],
        out_shape=[jax.ShapeDtypeStruct((nb, 1, GROUP_W), F32),
                   jax.ShapeDtypeStruct(state.shape, F32)],
        compiler_params=_params(("parallel",)),
        name="retention_sample",
    )(gam, rq, rk, rv, rg, state)


def _diff_prompt_kernel(qi_ref, kj_ref, sc_ref, q_ref, k_ref, v_ref, b_ref, o_ref,
                        q2_ref, m_ref, l_ref, acc_ref, *, out_scale):
    h = pl.program_id(1)
    step = pl.program_id(2)
    i = qi_ref[step]
    j = kj_ref[step]
    t = q_ref.shape[1]
    rows = 2 * t // ATTN_ROW_CHUNKS

    @pl.when(j == 0)
    def _():
        q = q_ref[0]
        lane = lax.broadcasted_iota(jnp.int32, q.shape, 1)
        zero = jnp.zeros_like(q)
        q2_ref[0:t, :] = jnp.where(lane < DIFF_DH, q, zero)
        q2_ref[t:2 * t, :] = jnp.where(lane < DIFF_DH, zero, q)
        m_ref[...] = jnp.full(m_ref.shape, NEG_INF, F32)
        l_ref[...] = jnp.zeros(l_ref.shape, F32)
        acc_ref[...] = jnp.zeros(acc_ref.shape, F32)

    def update(bias_of):
        k = k_ref[0]
        v = v_ref[0]
        for c in range(ATTN_ROW_CHUNKS):
            rs = slice(c * rows, (c + 1) * rows)
            s = _dot_nt(q2_ref[rs, :], k) + bias_of((c * rows) % t, rows)
            m_prev = m_ref[rs, :]
            m_new = jnp.maximum(m_prev, jnp.max(s, axis=-1, keepdims=True))
            alpha = jnp.exp(m_prev - m_new)
            p = jnp.exp(s - m_new)
            l_ref[rs, :] = alpha * l_ref[rs, :] + jnp.sum(p, axis=-1, keepdims=True)
            acc_ref[rs, :] = alpha * acc_ref[rs, :] + _dot(p.astype(BF16), v)
            m_ref[rs, :] = m_new

    d = i - j

    @pl.when(d >= 2)
    def _():
        far = sc_ref[h]
        update(lambda r0, n: far)

    @pl.when(d == 1)
    def _():
        update(lambda r0, n: b_ref[0, 1, r0:r0 + n, :])

    @pl.when(d == 0)
    def _():
        update(lambda r0, n: b_ref[0, 0, r0:r0 + n, :])
        lam = sc_ref[DIFF_HEADS]
        o = acc_ref[...] / l_ref[...]
        o = o[0:t] - lam * o[t:2 * t]
        o = o * lax.rsqrt(jnp.mean(o * o, axis=-1, keepdims=True) + EPS)
        o_ref[0] = (o * out_scale).astype(BF16)


def _diff_prompt(dqb, dkb, dvb, bias_tiles, scalars, out_scale):
    b, s, w = dqb.shape
    t = bias_tiles.shape[-1]
    n = s // t
    pairs = [(i, j) for i in range(n) for j in range(i + 1)]
    qi = jnp.asarray([p[0] for p in pairs], jnp.int32)
    kj = jnp.asarray([p[1] for p in pairs], jnp.int32)
    qspec = pl.BlockSpec((1, t, HEAD_W), lambda b_, h, st, qi_, kj_: (b_, qi_[st], h))
    kspec = pl.BlockSpec((1, t, HEAD_W), lambda b_, h, st, qi_, kj_: (b_, kj_[st], h))
    grid_spec = pltpu.PrefetchScalarGridSpec(
        num_scalar_prefetch=2,
        grid=(b, DIFF_HEADS, len(pairs)),
        in_specs=[pl.BlockSpec(memory_space=pltpu.SMEM), qspec, kspec, kspec,
                  pl.BlockSpec((1, 2, t, t), lambda b_, h, st, qi_, kj_: (h, 0, 0, 0))],
        out_specs=qspec,
        scratch_shapes=[pltpu.VMEM((2 * t, HEAD_W), BF16), pltpu.VMEM((2 * t, 1), F32),
                        pltpu.VMEM((2 * t, 1), F32), pltpu.VMEM((2 * t, HEAD_W), F32)],
    )
    return pl.pallas_call(
        functools.partial(_diff_prompt_kernel, out_scale=out_scale),
        grid_spec=grid_spec,
        out_shape=jax.ShapeDtypeStruct((b, s, w), BF16),
        compiler_params=_params(("parallel", "parallel", "arbitrary")),
        name="diff_attn_prompt",
    )(qi, kj, scalars, dqb, dkb, dvb, bias_tiles)


def _diff_sample_kernel(pt_ref, sc_ref, q_ref, kn_ref, vn_ref, bias_ref, *rest, n_pp, out_scale):
    k_refs = rest[:n_pp]
    v_refs = rest[n_pp:2 * n_pp]
    o_ref, qm_ref, m_ref, l_ref, acc_ref = rest[2 * n_pp:]
    j = pl.program_id(1)
    n_rows = 2 * DIFF_HEADS

    @pl.when(j == 0)
    def _():
        q = jnp.broadcast_to(q_ref[0], (n_rows, GROUP_W))
        lane = lax.broadcasted_iota(jnp.int32, (n_rows, GROUP_W), 1)
        row = lax.broadcasted_iota(jnp.int32, (n_rows, GROUP_W), 0)
        qm = jnp.where(lax.shift_right_logical(lane, int(math.log2(DIFF_DH))) == row, q, 0.0)
        qm_ref[...] = qm
        kn = kn_ref[0].astype(BF16).astype(F32)
        s_new = jnp.sum(qm * kn, axis=-1, keepdims=True)
        head = lax.shift_right_logical(lax.broadcasted_iota(jnp.int32, (n_rows, 1), 0), 1)
        b0 = jnp.zeros((n_rows, 1), F32)
        for hh in range(DIFF_HEADS):
            b0 = jnp.where(head == hh, sc_ref[DIFF_HEADS + 1 + hh], b0)
        m_ref[...] = s_new + b0
        l_ref[...] = jnp.ones(l_ref.shape, F32)
        acc_ref[...] = jnp.broadcast_to(vn_ref[0].astype(BF16).astype(F32), (n_rows, GROUP_W))

    qm = qm_ref[...].astype(BF16)
    s = jnp.concatenate([_dot(qm, k_refs[u][0].astype(BF16)) for u in range(n_pp)], axis=1)
    s = s + bias_ref[...]
    m_prev = m_ref[...]
    m_new = jnp.maximum(m_prev, jnp.max(s, axis=-1, keepdims=True))
    alpha = jnp.exp(m_prev - m_new)
    p = jnp.exp(s - m_new)
    l_ref[...] = alpha * l_ref[...] + jnp.sum(p, axis=-1, keepdims=True)
    pb = p.astype(BF16)
    for hh in range(DIFF_HEADS):
        hs = slice(hh * HEAD_W, (hh + 1) * HEAD_W)
        pv = None
        for u in range(n_pp):
            v_h = v_refs[u][0, pl.ds(hh, PAGE, stride=DIFF_HEADS), :].astype(BF16)
            d = _dot(pb[:, u * PAGE:(u + 1) * PAGE], v_h)
            pv = d if pv is None else pv + d
        acc_ref[:, hs] = alpha * acc_ref[:, hs] + pv
    m_ref[...] = m_new

    @pl.when(j == pl.num_programs(1) - 1)
    def _():
        lam = sc_ref[DIFF_HEADS]
        o = acc_ref[...] / l_ref[...]
        for hh in range(DIFF_HEADS):
            hs = slice(hh * HEAD_W, (hh + 1) * HEAD_W)
            oh = o[2 * hh:2 * hh + 1, hs] - lam * o[2 * hh + 1:2 * hh + 2, hs]
            oh = oh * lax.rsqrt(jnp.mean(oh * oh, axis=-1, keepdims=True) + EPS)
            o_ref[0, :, hs] = oh * out_scale


def _diff_sample(dqb, dk_new, dv_new, cache_k, cache_v, page_table, bias_past, scalars, out_scale):
    nb = dqb.shape[0]
    n_pages = page_table.shape[1]
    n_pp = PAGES_PER_STEP
    while n_pages % n_pp:
        n_pp //= 2
    tok = pl.BlockSpec((1, 1, GROUP_W), lambda b, j, pt: (b, 0, 0))

    def page_spec(u):
        return pl.BlockSpec((1, GROUP_W, PAGE), lambda b, j, pt: (pt[b, j * n_pp + u], 0, 0))

    n_rows = 2 * DIFF_HEADS
    grid_spec = pltpu.PrefetchScalarGridSpec(
        num_scalar_prefetch=1,
        grid=(nb, n_pages // n_pp),
        in_specs=[pl.BlockSpec(memory_space=pltpu.SMEM), tok, tok, tok,
                  pl.BlockSpec((n_rows, n_pp * PAGE), lambda b, j, pt: (0, j))]
                 + [page_spec(u) for u in range(n_pp)] * 2,
        out_specs=tok,
        scratch_shapes=[pltpu.VMEM((n_rows, GROUP_W), F32), pltpu.VMEM((n_rows, 1), F32),
                        pltpu.VMEM((n_rows, 1), F32), pltpu.VMEM((n_rows, GROUP_W), F32)],
    )
    return pl.pallas_call(
        functools.partial(_diff_sample_kernel, n_pp=n_pp, out_scale=out_scale),
        grid_spec=grid_spec,
        out_shape=jax.ShapeDtypeStruct((nb, 1, GROUP_W), F32),
        compiler_params=_params(("parallel", "arbitrary")),
        name="diff_attn_sample",
    )(page_table, scalars, dqb, dk_new, dv_new, bias_past,
      *([cache_k] * n_pp), *([cache_v] * n_pp))


def _mem_kv_kernel(m_ref, g_ref, wk_ref, wv_ref, k_ref, v_ref, kb_ref, vb_ref):
    mn = _rms(m_ref[...], g_ref[...]).astype(BF16)
    k = _dot(mn, wk_ref[...])
    v = _dot(mn, wv_ref[...])
    k_ref[...] = k
    v_ref[...] = v
    kb_ref[...] = k.astype(BF16)
    vb_ref[...] = v.astype(BF16)


def _mem_kv(mem, g, wk_bf, wv_bf, tm):
    n, d = mem.shape
    row = pl.BlockSpec((tm, d), lambda i: (i, 0))
    fix = lambda i: (0, 0)
    return pl.pallas_call(
        _mem_kv_kernel,
        grid=(n // tm,),
        in_specs=[row, pl.BlockSpec((1, d), fix), pl.BlockSpec((d, d), fix), pl.BlockSpec((d, d), fix)],
        out_specs=[row] * 4,
        out_shape=[jax.ShapeDtypeStruct((n, d), F32)] * 2 + [jax.ShapeDtypeStruct((n, d), BF16)] * 2,
        compiler_params=_params(("parallel",)),
        name="mem_kv",
    )(mem, g, wk_bf, wv_bf)


def _mix_residual(x, ret, dif, w_out_ref):
    half = w_out_ref.shape[0] // 2
    return x + _dot(ret.astype(BF16), w_out_ref[0:half, :]) + _dot(dif.astype(BF16), w_out_ref[half:, :])


def _cross_heads(q, mem_k, mem_v, o_ref):
    dh = q.shape[1] // MEM_HEADS
    for h in range(MEM_HEADS):
        hs = slice(h * dh, (h + 1) * dh)
        s = _dot_nt(q[:, hs], mem_k(h)) * (dh ** -0.5)
        e = jnp.exp(s - jnp.max(s, axis=-1, keepdims=True))
        p = e / jnp.sum(e, axis=-1, keepdims=True)
        o_ref[:, hs] = _dot(p.astype(BF16), mem_v(h)).astype(o_ref.dtype)


def _post_mix_kernel(x_ref, ret_ref, dif_ref, mk_ref, mv_ref, wo_ref, gc_ref, wq_ref, wm_ref, gf_ref,
                     h_ref, xt_ref, o_scr):
    h1 = _mix_residual(x_ref[...], ret_ref[...], dif_ref[...], wo_ref)
    q = _dot(_rms(h1, gc_ref[...]).astype(BF16), wq_ref[...]).astype(BF16)
    dh = q.shape[1] // MEM_HEADS
    _cross_heads(q, lambda h: mk_ref[0, :, h * dh:(h + 1) * dh], lambda h: mv_ref[0, :, h * dh:(h + 1) * dh],
                 o_scr)
    h2 = h1 + _dot(o_scr[...], wm_ref[...])
    h_ref[...] = h2
    xt_ref[...] = _rms(h2, gf_ref[...]).T.astype(BF16)


def _post_mix(x, ret, dif, mk_bf, mv_bf, wo_bf, g_cross, wq_bf, wm_bf, g_ffn, tm):
    n, d = x.shape
    nb, m, _ = mk_bf.shape
    per_batch = n // nb // tm
    row = lambda i: (i, 0)
    fix = lambda i: (0, 0)
    mem = pl.BlockSpec((1, m, d), lambda i: (i // per_batch, 0, 0))
    half = pl.BlockSpec((tm, GROUP_W), row)
    return pl.pallas_call(
        _post_mix_kernel,
        grid=(n // tm,),
        in_specs=[pl.BlockSpec((tm, d), row), half, half, mem, mem,
                  pl.BlockSpec((d, d), fix), pl.BlockSpec((1, d), fix), pl.BlockSpec((d, d), fix),
                  pl.BlockSpec((d, d), fix), pl.BlockSpec((1, d), fix)],
        out_specs=[pl.BlockSpec((tm, d), row), pl.BlockSpec((d, tm), lambda i: (0, i))],
        out_shape=[jax.ShapeDtypeStruct((n, d), F32), jax.ShapeDtypeStruct((d, n), BF16)],
        scratch_shapes=[pltpu.VMEM((tm, d), BF16)],
        compiler_params=_params(("parallel",)),
        name="post_mix",
    )(x, ret, dif, mk_bf, mv_bf, wo_bf, g_cross, wq_bf, wm_bf, g_ffn)


def _sample_mix_kernel(x_ref, ret_ref, dif_ref, wo_ref, gc_ref, wq_ref, h_ref, q_ref):
    h1 = _mix_residual(x_ref[...], ret_ref[...], dif_ref[...], wo_ref)
    h_ref[...] = h1
    q_ref[...] = _dot(_rms(h1, gc_ref[...]).astype(BF16), wq_ref[...])


def _sample_cross_kernel(q_ref, mk_ref, mv_ref, o_ref, o_scr):
    q8 = jnp.broadcast_to(q_ref[0], (8, q_ref.shape[2])).astype(BF16)
    dh = q8.shape[1] // MEM_HEADS
    _cross_heads(q8, lambda h: mk_ref[0, :, h * dh:(h + 1) * dh].astype(BF16),
                 lambda h: mv_ref[0, :, h * dh:(h + 1) * dh].astype(BF16), o_scr)
    o_ref[0] = o_scr[0:1, :]


def _sample_out_kernel(h_ref, o_ref, wm_ref, gf_ref, h2_ref, xt_ref, *, lanes):
    h2 = h_ref[...] + _dot(o_ref[...].astype(BF16), wm_ref[...])
    h2_ref[...] = h2
    xb = _rms(h2, gf_ref[...])
    pad = jnp.zeros((lanes - xb.shape[0], xb.shape[1]), F32)
    xt_ref[...] = jnp.concatenate([xb, pad], axis=0).T.astype(BF16)


def _post_mix_sample(x, ret, dif, mem_k, mem_v, wo_bf, g_cross, wq_bf, wm_bf, g_ffn, lanes):
    n, d = x.shape
    whole = lambda shape: pl.BlockSpec(shape, lambda: tuple(0 for _ in shape))
    h1, q = pl.pallas_call(
        _sample_mix_kernel,
        in_specs=[whole((n, d)), whole((n, GROUP_W)), whole((n, GROUP_W)), whole((d, d)),
                  whole((1, d)), whole((d, d))],
        out_specs=[whole((n, d)), whole((n, d))],
        out_shape=[jax.ShapeDtypeStruct((n, d), F32), jax.ShapeDtypeStruct((n, d), F32)],
        compiler_params=_params(()),
        name="sample_mix",
    )(x, ret, dif, wo_bf, g_cross, wq_bf)
    tok = pl.BlockSpec((1, 1, d), lambda b: (b, 0, 0))
    mem = pl.BlockSpec((1,) + mem_k.shape[1:], lambda b: (b, 0, 0))
    o = pl.pallas_call(
        _sample_cross_kernel,
        grid=(n,),
        in_specs=[tok, mem, mem],
        out_specs=tok,
        out_shape=jax.ShapeDtypeStruct((n, 1, d), F32),
        scratch_shapes=[pltpu.VMEM((8, d), F32)],
        compiler_params=_params(("parallel",)),
        name="sample_cross",
    )(q.reshape(n, 1, d), mem_k, mem_v)
    return pl.pallas_call(
        functools.partial(_sample_out_kernel, lanes=lanes),
        in_specs=[whole((n, d)), whole((n, d)), whole((d, d)), whole((1, d))],
        out_specs=[whole((n, d)), whole((d, lanes))],
        out_shape=[jax.ShapeDtypeStruct((n, d), F32), jax.ShapeDtypeStruct((d, lanes), BF16)],
        compiler_params=_params(()),
        name="sample_out",
    )(h1, o.reshape(n, d), wm_bf, g_ffn)


def _candidate_tiles():
    tiles = [("j", 0, 0, ()), ("j", 0, 8, ()), ("j", 1, 0, ())]
    tiles += [("i", 0, 0, (0, 1)), ("i", 0, 8, ())]
    tiles += [("i", j, 0, (0, 1)) for j in (1, 2, 3, 4)]
    return tiles


def _top_k_rows(s, k):
    n, t = s.shape
    rows = lax.broadcasted_iota(jnp.int32, s.shape, 0).astype(F32)
    sub = lax.broadcasted_iota(jnp.int32, (8, t), 0)
    rank = jnp.full(s.shape, float(k), F32)
    blocks = [jnp.zeros((8, t), F32) for _ in range(k // 8)]
    for it in range(k):
        m = jnp.max(s, axis=0, keepdims=True)
        first = jnp.min(jnp.where(s == m, rows, float(n)), axis=0, keepdims=True)
        sel = rows == first
        rank = jnp.where(sel, float(it), rank)
        s = jnp.where(sel, -jnp.inf, s)
        blocks[it // 8] = jnp.where(sub == it % 8, m, blocks[it // 8])
    return blocks, rank


def _route_lanes(s1, s2):
    k = PEER_TOPK
    t = s1.shape[1]
    tiles = _candidate_tiles()
    sub = lax.broadcasted_iota(jnp.int32, (8, t), 0)
    a, rank1 = _top_k_rows(s1, k)
    b, rank2 = _top_k_rows(s2, k)

    def row_of(blocks, i):
        return blocks[i // 8][i % 8:i % 8 + 1, :]

    cands, poss = [], []
    for kind, fixed, start, skip in tiles:
        if kind == "j":
            c = row_of(a, fixed) + b[start // 8]
            pos = fixed * k + start + sub
        else:
            c = a[start // 8] + row_of(b, fixed)
            pos = (start + sub) * k + fixed
            for r in skip:
                c = jnp.where(sub == r, -jnp.inf, c)
                pos = jnp.where(sub == r, k * k, pos)
        cands.append(c)
        poss.append(pos.astype(F32))
    c_all = jnp.concatenate(cands, axis=0)
    pos_all = jnp.concatenate(poss, axis=0)
    top = c_all[0:1, :]
    picked = jnp.zeros(c_all.shape, F32)
    work = c_all
    for _ in range(k):
        m = jnp.max(work, axis=0, keepdims=True)
        first = jnp.min(jnp.where(work == m, pos_all, float(k * k)), axis=0, keepdims=True)
        sel = pos_all == first
        picked = jnp.where(sel, 1.0, picked)
        work = jnp.where(sel, -jnp.inf, work)
    z = jnp.sum(picked * jnp.exp(jnp.where(picked > 0, c_all, top) - top), axis=0, keepdims=True)

    cnt_lo = jnp.zeros((8, t), F32)
    cnt_hi = jnp.zeros((8, t), F32)
    for n_tile, (kind, fixed, start, skip) in enumerate(tiles):
        pk = picked[n_tile * 8:(n_tile + 1) * 8, :]
        if kind == "j":
            tot = jnp.sum(pk, axis=0, keepdims=True)
            add = jnp.where(sub == fixed % 8, tot, 0.0)
            if fixed < 8:
                cnt_lo = cnt_lo + add
            else:
                cnt_hi = cnt_hi + add
        elif start == 0:
            cnt_lo = cnt_lo + pk
        else:
            cnt_hi = cnt_hi + pk
    cnt = [cnt_lo, cnt_hi]

    cnt1 = jnp.zeros(s1.shape, F32)
    for i in range(k):
        cnt1 = jnp.where(rank1 == float(i), row_of(cnt, i), cnt1)
    return rank2, jnp.exp(s2 - row_of(b, 0)), cnt1, jnp.exp(s1 - row_of(a, 0)) / z


def _route_kernel(xt_ref, wq_ref, keys_ref, rank2_ref, e2w_ref, cnt1_ref, e1w_ref, q_scr):
    q_scr[...] = _dot(wq_ref[...], xt_ref[...]).astype(BF16)
    t = xt_ref.shape[1]
    lane_w = 128

    def head(h, carry):
        base = pl.multiple_of(h * PEER_DQ, PEER_DQ)
        s1 = _dot(keys_ref[h, 0], q_scr[pl.ds(base, PEER_DQ // 2), :])
        s2 = _dot(keys_ref[h, 1], q_scr[pl.ds(base + PEER_DQ // 2, PEER_DQ // 2), :])
        for g in range(t // lane_w):
            ls = slice(g * lane_w, (g + 1) * lane_w)
            rank2, e2w, cnt1, e1w = _route_lanes(s1[:, ls], s2[:, ls])
            rank2_ref[h, :, ls] = rank2.astype(BF16)
            e2w_ref[h, :, ls] = e2w.astype(BF16)
            cnt1_ref[h, :, ls] = cnt1
            e1w_ref[h, :, ls] = e1w
        return carry

    lax.fori_loop(0, PEER_HEADS, head, 0)


def _route(xt, wq_t, keys_bf):
    d, n = xt.shape
    t = min(ROUTE_TILE, n)
    table = pl.BlockSpec((PEER_HEADS, N_KEYS, t), lambda i: (0, 0, i))
    shape = jax.ShapeDtypeStruct((PEER_HEADS, N_KEYS, n), F32)
    half_shape = jax.ShapeDtypeStruct((PEER_HEADS, N_KEYS, n), BF16)
    return pl.pallas_call(
        _route_kernel,
        grid=(n // t,),
        in_specs=[pl.BlockSpec((d, t), lambda i: (0, i)),
                  pl.BlockSpec(wq_t.shape, lambda i: (0, 0)),
                  pl.BlockSpec(keys_bf.shape, lambda i: (0, 0, 0, 0))],
        out_specs=[table] * 4,
        out_shape=[half_shape] * 2 + [shape] * 2,
        scratch_shapes=[pltpu.VMEM((PEER_HEADS * PEER_DQ, t), BF16)],
        compiler_params=_params(("parallel",)),
        name="peer_route",
    )(xt, wq_t, keys_bf)


def _peer_gate(a_ref, hid_ref, tile, rank2_ref, e2w_ref, cnt1_ref, e1w_ref):
    te, tm = a_ref.shape
    lane_w = 128
    zero = jnp.zeros((N_KEYS, lane_w), BF16)
    for r in range(te // N_KEYS):
        rs = slice(r * N_KEYS, (r + 1) * N_KEYS)
        e1_blk = pl.multiple_of(tile * (te // N_KEYS) + 8 * (r // 8), 8)
        for c in range(tm // lane_w):
            ls = slice(c * lane_w, (c + 1) * lane_w)
            gate = zero
            for h in range(PEER_HEADS):
                cnt = cnt1_ref[h, pl.ds(e1_blk, 8), ls][r % 8:r % 8 + 1, :]
                w1 = e1w_ref[h, pl.ds(e1_blk, 8), ls][r % 8:r % 8 + 1, :]
                cnt = jnp.broadcast_to(cnt, (N_KEYS, lane_w)).astype(BF16)
                w1 = jnp.broadcast_to(w1, (N_KEYS, lane_w)).astype(BF16)
                gate = gate + jnp.where(rank2_ref[h, :, ls] < cnt, e2w_ref[h, :, ls], zero) * w1
            a = a_ref[rs, ls]
            act = 0.5 * a * (1.0 + lax.erf(a * (2.0 ** -0.5)))
            hid_ref[rs, ls] = act.astype(BF16) * gate


def _peer_kernel(xt_ref, u_first_ref, u_odd_ref, u_next_ref, vt_prev_ref, vt_even_ref, vt_last_ref,
                 rank2_in_ref, e2w_in_ref, cnt1_ref, e1w_ref, h_ref, g_ref,
                 y_ref, acc_ref, a0_ref, a1_ref, hid0_ref, hid1_ref, rank2_ref, e2w_ref):
    s = pl.program_id(1)
    tables = (rank2_ref, e2w_ref, cnt1_ref, e1w_ref)
    xt = xt_ref[...]

    @pl.when(s == 0)
    def _():
        acc_ref[...] = jnp.zeros(acc_ref.shape, F32)
        hid1_ref[...] = jnp.zeros(hid1_ref.shape, BF16)
        a0_ref[...] = _dot(u_first_ref[...], xt)
        rank2_ref[...] = rank2_in_ref[...]
        e2w_ref[...] = e2w_in_ref[...]

    _peer_gate(a0_ref, hid0_ref, 2 * s, *tables)
    out = _dot(vt_prev_ref[...], hid1_ref[...])
    a1_ref[...] = _dot(u_odd_ref[...], xt)
    _peer_gate(a1_ref, hid1_ref, 2 * s + 1, *tables)
    out = out + _dot(vt_even_ref[...], hid0_ref[...])
    a0_ref[...] = _dot(u_next_ref[...], xt)
    acc_ref[...] += out

    @pl.when(s == pl.num_programs(1) - 1)
    def _():
        acc = acc_ref[...] + _dot(vt_last_ref[...], hid1_ref[...])
        y_ref[...] = _rms(h_ref[...] + acc.T, g_ref[...])


def _peer(xt, u_bf, vt_bf, tables, h2, g_final, tm):
    d, n = xt.shape
    ne = u_bf.shape[0]
    te = PEER_EXPERT_TILE
    n_tiles = ne // te
    n_steps = n_tiles // 2
    assert n_tiles == 2 * n_steps
    table = pl.BlockSpec((PEER_HEADS, N_KEYS, tm), lambda i, s: (0, 0, i))
    once = pl.Buffered(1)
    u_spec = lambda tile, mode=None: pl.BlockSpec((te, d), lambda i, s: (tile(s), 0), pipeline_mode=mode)
    vt_spec = lambda tile, mode=None: pl.BlockSpec((d, te), lambda i, s: (0, tile(s)), pipeline_mode=mode)
    return pl.pallas_call(
        _peer_kernel,
        grid=(n // tm, n_steps),
        in_specs=[pl.BlockSpec((d, tm), lambda i, s: (0, i)),
                  u_spec(lambda s: 0, once),
                  u_spec(lambda s: 2 * s + 1),
                  u_spec(lambda s: jnp.minimum(2 * s + 2, n_tiles - 1)),
                  vt_spec(lambda s: jnp.maximum(2 * s - 1, 0)),
                  vt_spec(lambda s: 2 * s),
                  vt_spec(lambda s: n_tiles - 1, once),
                  table, table, table, table,
                  pl.BlockSpec((tm, d), lambda i, s: (i, 0)),
                  pl.BlockSpec((1, d), lambda i, s: (0, 0))],
        out_specs=pl.BlockSpec((tm, d), lambda i, s: (i, 0)),
        out_shape=jax.ShapeDtypeStruct((n, d), F32),
        scratch_shapes=[pltpu.VMEM((d, tm), F32), pltpu.VMEM((te, tm), F32), pltpu.VMEM((te, tm), F32),
                        pltpu.VMEM((te, tm), BF16), pltpu.VMEM((te, tm), BF16),
                        pltpu.VMEM((PEER_HEADS, N_KEYS, tm), BF16), pltpu.VMEM((PEER_HEADS, N_KEYS, tm), BF16)],
        compiler_params=_params(("parallel", "arbitrary"), PEER_VMEM_LIMIT),
        name="peer_dense",
    )(xt, u_bf, u_bf, u_bf, vt_bf, vt_bf, vt_bf, *tables, h2, g_final)


def _t5_bias(rel, table):
    n = jnp.maximum(rel, 0)
    max_exact = N_BUCKETS // 2
    nf = jnp.maximum(n, 1).astype(F32)
    large = max_exact + (jnp.log(nf / max_exact) / math.log(MAX_DISTANCE / max_exact)
                         * (N_BUCKETS - max_exact)).astype(jnp.int32)
    large = jnp.minimum(large, N_BUCKETS - 1)
    bucket = jnp.where(n < max_exact, n, large)
    return jnp.moveaxis(table[bucket].astype(F32), -1, 0)


def _toeplitz(vec, t):
    hh = vec.shape[0]
    flat = jnp.tile(vec, (1, t))[:, :t * (2 * t - 1)]
    return flat.reshape(hh, t, 2 * t - 1)[:, :, :t]


def _prompt_bias_tiles(rel_bias, t):
    k = jnp.arange(2 * t)
    tiles = []
    for d in (0, 1):
        rel = jnp.where(k < t, d * t - k, d * t + 2 * t - k)
        vec = _t5_bias(rel, rel_bias)
        vec = jnp.where(rel[None, :] >= 0, vec, NEG_INF)
        tiles.append(_toeplitz(vec, t))
    return jnp.stack(tiles, axis=1)


def _rotary_tables(pos):
    half = RET_DK // 2
    inv = 1.0 / (ROPE_BASE ** (jnp.arange(half, dtype=F32) / half))
    ang = pos.astype(F32)[:, None] * inv[None, :]
    cos, sin = jnp.cos(ang), jnp.sin(ang)
    return jnp.concatenate([cos, cos], axis=1), jnp.concatenate([-sin, sin], axis=1)


def _retention_tables(log_gamma):
    n = jnp.arange(RET_CHUNK, dtype=F32)
    diff = n[:, None] - n[None, :]
    decay = jnp.where(diff[None] >= 0,
                      jnp.exp(jnp.maximum(diff, 0.0)[None] * log_gamma[:, None, None]), 0.0)
    ones = jnp.ones((1, 1, HEAD_W), F32)
    qdec = jnp.exp((n + 1.0)[None, :] * log_gamma[:, None])[:, :, None] * ones
    kdec = jnp.exp((RET_CHUNK - 1.0 - n)[None, :] * log_gamma[:, None])[:, :, None] * ones
    return decay, qdec, kdec, jnp.exp(RET_CHUNK * log_gamma)


def kernel(x_prompt, x_sample, mem_prompt, cache_diff_k, cache_diff_v, page_table, state_ret,
           cache_mem_k, cache_mem_v, g_mix, w_in, w_out, lam_q1, lam_k1, lam_q2, lam_k2, rel_bias,
           g_cross, g_mem, w_mq, w_mk, w_mv, w_mo, g_ffn, w_pq, sub_keys, peer_u, peer_v, g_final):
    depth = g_mix.shape[0]
    assert depth == 1
    l = 0
    b, s, d = x_prompt.shape
    db, t_s, _ = x_sample.shape
    assert t_s == 1
    n_pages = page_table.shape[1]
    past = n_pages * PAGE
    mem_len = mem_prompt.shape[1]
    n_tok = b * s

    log_gamma = jnp.log(1.0 - 2.0 ** (-5.0 - jnp.arange(RET_HEADS, dtype=F32)))
    lam_init = 0.8 - 0.6 * math.exp(-0.3 * l)
    lam = (jnp.exp(jnp.sum(lam_q1[l].astype(F32) * lam_k1[l].astype(F32)))
           - jnp.exp(jnp.sum(lam_q2[l].astype(F32) * lam_k2[l].astype(F32))) + lam_init)
    out_scale = 1.0 - lam_init

    row = lambda v: v.reshape(1, -1)
    w_in_bf = w_in[l].astype(BF16)
    w_out_bf = w_out[l].astype(BF16)
    w_mq_bf = w_mq[l].astype(BF16)
    w_mk_bf = w_mk[l].astype(BF16)
    w_mv_bf = w_mv[l].astype(BF16)
    w_mo_bf = w_mo[l].astype(BF16)
    w_pq_t = w_pq[l].T.astype(BF16)
    keys_bf = sub_keys[l].astype(BF16)
    u_bf = peer_u[l].astype(BF16)
    vt_bf = peer_v[l].T.astype(BF16)
    g_fin = row(g_final)

    bias_far = _t5_bias(jnp.full((1,), MAX_DISTANCE, jnp.int32), rel_bias)[:, 0]
    bias_zero = _t5_bias(jnp.zeros((1,), jnp.int32), rel_bias)[:, 0]
    scalars = jnp.concatenate([bias_far, lam.reshape(1), bias_zero]).astype(F32)

    cs_p, sn_p = _rotary_tables(jnp.arange(s))
    rq, rk, rv, rg, dqb, dk, dv, dkb, dvb = _in_proj(
        x_prompt.reshape(n_tok, d), row(g_mix[l]), w_in_bf, cs_p, sn_p, TOKEN_TILE)
    decay, qdec, kdec, gl = _retention_tables(log_gamma)
    shp = (b, s, GROUP_W)
    ret_p, state_p = _retention_prompt(rq.reshape(shp), rk.reshape(shp), rv.reshape(shp),
                                       rg.reshape(shp), decay, qdec, kdec, gl)
    dif_p = _diff_prompt(dqb.reshape(shp), dkb.reshape(shp), dvb.reshape(shp),
                         _prompt_bias_tiles(rel_bias, ATTN_TILE), scalars, out_scale)
    mk, mv, mk_bf, mv_bf = _mem_kv(mem_prompt.reshape(b * mem_len, d), row(g_mem[l]),
                                   w_mk_bf, w_mv_bf, mem_len)
    h2_p, xt_p = _post_mix(x_prompt.reshape(n_tok, d), ret_p.reshape(n_tok, GROUP_W),
                           dif_p.reshape(n_tok, GROUP_W), mk_bf.reshape(b, mem_len, d),
                           mv_bf.reshape(b, mem_len, d), w_out_bf, row(g_cross[l]), w_mq_bf,
                           w_mo_bf, row(g_ffn[l]), TOKEN_TILE)
    tables_p = _route(xt_p, w_pq_t, keys_bf)
    y_p = _peer(xt_p, u_bf, vt_bf, tables_p, h2_p, g_fin, TOKEN_TILE)

    lanes = SAMPLE_LANES
    assert db <= lanes
    cs_s, sn_s = _rotary_tables(jnp.full((db,), past, jnp.int32))
    xs = x_sample.reshape(db, d)
    rq_s, rk_s, rv_s, rg_s, dqb_s, dk_s, dv_s, _, _ = _in_proj(
        xs, row(g_mix[l]), w_in_bf, cs_s, sn_s, db)
    tok = (db, 1, GROUP_W)
    ret_s, state_s = _retention_sample(rq_s.reshape(tok), rk_s.reshape(tok), rv_s.reshape(tok),
                                       rg_s.reshape(tok), state_ret[l], jnp.exp(log_gamma))
    rel_past = past - jnp.arange(past)
    bias_past = jnp.repeat(_t5_bias(rel_past, rel_bias), 2, axis=0)
    dif_s = _diff_sample(dqb_s.astype(F32).reshape(tok), dk_s.reshape(tok), dv_s.reshape(tok),
                         jnp.transpose(cache_diff_k[l], (0, 2, 3, 4, 1)).reshape(-1, GROUP_W, PAGE),
                         cache_diff_v[l].reshape(-1, PAGE * DIFF_HEADS, HEAD_W),
                         page_table, bias_past, scalars, out_scale)
    h2_s, xt_s = _post_mix_sample(xs, ret_s.reshape(db, GROUP_W), dif_s.reshape(db, GROUP_W),
                                  cache_mem_k[l].reshape(db, mem_len, d),
                                  cache_mem_v[l].reshape(db, mem_len, d),
                                  w_out_bf, row(g_cross[l]), w_mq_bf, w_mo_bf, row(g_ffn[l]), lanes)
    tables_s = _route(xt_s, w_pq_t, keys_bf)
    h2_pad = jnp.concatenate([h2_s, jnp.zeros((lanes - db, d), F32)], axis=0)
    y_s = _peer(xt_s, u_bf, vt_bf, tables_s, h2_pad, g_fin, lanes)[:db]

    return (y_p.reshape(b, s, d), y_s.reshape(db, 1, d),
            dk.reshape(1, b, s, DIFF_HEADS, 2, DIFF_DH), dv.reshape(1, b, s, DIFF_HEADS, HEAD_W),
            state_p[None], mk.reshape(1, b, mem_len, MEM_HEADS, d // MEM_HEADS),
            mv.reshape(1, b, mem_len, MEM_HEADS, d // MEM_HEADS),
            dk_s.reshape(1, db, 1, DIFF_HEADS, 2, DIFF_DH), dv_s.reshape(1, db, 1, DIFF_HEADS, HEAD_W),
            state_s[None])
```

```python
import functools
import math

import jax
import jax.numpy as jnp
import numpy as np
from jax import lax
from jax.experimental import pallas as pl
from jax.experimental.pallas import tpu as pltpu

F32 = jnp.float32
BF16 = jnp.bfloat16

EPS = 1e-6
NEG_INF = -1e30
ROPE_BASE = 10000.0
RET_HEADS = 4
RET_DK = 128
RET_CHUNK = 128
DIFF_HEADS = 4
DIFF_DH = 64
HEAD_W = 128
GROUP_W = 512
N_GROUPS = 7
PAGE = 128
N_BUCKETS = 32
MAX_DISTANCE = 128
MEM_HEADS = 4
PEER_HEADS = 8
N_KEYS = 128
PEER_TOPK = 16
PEER_DQ = 256

VMEM_LIMIT = 56 * 1024 * 1024
ATTN_TILE = 512
ATTN_ROW_CHUNKS = 4
TOKEN_TILE = 512
PEER_EXPERT_TILE = 512
PEER_TOKEN_TILE = 1024
PEER_VMEM_LIMIT = 60 * 1024 * 1024
ROUTE_TILE = 256
SAMPLE_LANES = 128
PAGES_PER_STEP = 8


def _params(semantics, vmem=VMEM_LIMIT):
    return pltpu.CompilerParams(dimension_semantics=semantics, vmem_limit_bytes=vmem)


def _rms(x, g):
    return (x * lax.rsqrt(jnp.mean(x * x, axis=-1, keepdims=True) + EPS)) * g


def _dot(a, b):
    return jnp.dot(a, b, preferred_element_type=F32)


def _dot_nt(a, b):
    return lax.dot_general(a, b, (((1,), (1,)), ((), ())), preferred_element_type=F32)


def _in_proj_kernel(x_ref, g_ref, w_ref, cs_ref, sn_ref,
                    rq_ref, rk_ref, rv_ref, rg_ref, dqb_ref, dk_ref, dv_ref, dkb_ref, dvb_ref):
    xb = _rms(x_ref[...], g_ref[...]).astype(BF16)
    cs = cs_ref[...]
    sn = sn_ref[...]

    def proj(c):
        return _dot(xb, w_ref[:, c * GROUP_W:(c + 1) * GROUP_W])

    def rotary(p, out_ref, scale):
        for h in range(RET_HEADS):
            ph = p[:, h * HEAD_W:(h + 1) * HEAD_W]
            r = ph * cs + pltpu.roll(ph, HEAD_W // 2, 1) * sn
            if scale is not None:
                r = r * scale
            out_ref[:, h * HEAD_W:(h + 1) * HEAD_W] = r

    rotary(proj(0), rq_ref, RET_DK ** -0.5)
    rotary(proj(1), rk_ref, None)
    rv_ref[...] = proj(2)
    rg_ref[...] = proj(3)
    dqb_ref[...] = (proj(4) * (DIFF_DH ** -0.5)).astype(BF16)
    dk = proj(5)
    dk_ref[...] = dk
    dkb_ref[...] = dk.astype(BF16)
    dv = proj(6)
    dv_ref[...] = dv
    dvb_ref[...] = dv.astype(BF16)


def _in_proj(x, g, w_bf, cs, sn, tm):
    n, d = x.shape
    nt = n // tm
    n_pos = cs.shape[0] // tm
    row = lambda i: (i, 0)
    pos = lambda i: (i % n_pos, 0)
    fix = lambda i: (0, 0)
    wide = pl.BlockSpec((tm, GROUP_W), row)
    f32o = jax.ShapeDtypeStruct((n, GROUP_W), F32)
    bf16o = jax.ShapeDtypeStruct((n, GROUP_W), BF16)
    return pl.pallas_call(
        _in_proj_kernel,
        grid=(nt,),
        in_specs=[pl.BlockSpec((tm, d), row), pl.BlockSpec((1, d), fix),
                  pl.BlockSpec((d, N_GROUPS * GROUP_W), fix),
                  pl.BlockSpec((tm, HEAD_W), pos), pl.BlockSpec((tm, HEAD_W), pos)],
        out_specs=[wide] * 9,
        out_shape=[f32o, f32o, f32o, f32o, bf16o, f32o, f32o, bf16o, bf16o],
        compiler_params=_params(("parallel",)),
        name="in_proj",
    )(x, g, w_bf, cs, sn)


def _retention_kernel(gl_ref, rq_ref, rk_ref, rv_ref, rg_ref, dec_ref, qd_ref, kd_ref,
                      ret_ref, st_ref):
    c = pl.program_id(0)

    @pl.when(c == 0)
    def _():
        st_ref[...] = jnp.zeros(st_ref.shape, F32)

    nb = rq_ref.shape[0]
    for b in range(nb):
        for h in range(RET_HEADS):
            hs = slice(h * HEAD_W, (h + 1) * HEAD_W)
            q = rq_ref[b, :, hs].astype(BF16)
            k = rk_ref[b, :, hs]
            v = rv_ref[b, :, hs].astype(BF16)
            state = st_ref[b, h]
            scores = _dot_nt(q, k.astype(BF16)) * dec_ref[h]
            inner = _dot(scores.astype(BF16), v)
            cross = _dot(q, state.astype(BF16)) * qd_ref[h]
            o = inner + cross
            kd = (k * kd_ref[h]).T.astype(BF16)
            st_ref[b, h] = gl_ref[h] * state + _dot(kd, v)
            rg = rg_ref[b, :, hs]
            o = o * lax.rsqrt(jnp.mean(o * o, axis=-1, keepdims=True) + EPS)
            ret_ref[b, :, hs] = (o * (rg * jax.nn.sigmoid(rg))).astype(BF16)


def _retention_prompt(rq, rk, rv, rg, decay, qdec, kdec, gl):
    b, s, w = rq.shape
    nc = s // RET_CHUNK
    tok = pl.BlockSpec((b, RET_CHUNK, w), lambda c: (0, c, 0))
    const = pl.BlockSpec((RET_HEADS, RET_CHUNK, RET_CHUNK), lambda c: (0, 0, 0))
    return pl.pallas_call(
        _retention_kernel,
        grid=(nc,),
        in_specs=[pl.BlockSpec(memory_space=pltpu.SMEM), tok, tok, tok, tok, const, const, const],
        out_specs=[tok, pl.BlockSpec((b, RET_HEADS, RET_DK, HEAD_W), lambda c: (0, 0, 0, 0))],
        out_shape=[jax.ShapeDtypeStruct((b, s, w), BF16),
                   jax.ShapeDtypeStruct((b, RET_HEADS, RET_DK, HEAD_W), F32)],
        compiler_params=_params(("arbitrary",)),
        name="retention_prompt",
    )(gl, rq, rk, rv, rg, decay, qdec, kdec)


def _retention_step_kernel(gam_ref, rq_ref, rk_ref, rv_ref, rg_ref, st_ref, ret_ref, nst_ref):
    rows = lax.broadcasted_iota(jnp.int32, (RET_DK, HEAD_W), 0)
    for h in range(RET_HEADS):
        hs = slice(h * HEAD_W, (h + 1) * HEAD_W)
        q = rq_ref[0, :, hs]
        k = rk_ref[0, :, hs]
        v = rv_ref[0, :, hs]
        rg = rg_ref[0, :, hs]
        state = st_ref[0, h]
        gam = gam_ref[h]
        q8 = jnp.broadcast_to(q, (8, HEAD_W)).astype(BF16)
        qk = jnp.sum(q.astype(BF16).astype(F32) * k.astype(BF16).astype(F32), axis=-1, keepdims=True)
        cross = _dot(q8, state.astype(BF16))[0:1, :] * gam
        o = qk * v.astype(BF16).astype(F32) + cross
        k_first = jnp.where(rows == 0, jnp.broadcast_to(k, (RET_DK, HEAD_W)), 0.0)
        v_first = jnp.where(rows == 0, jnp.broadcast_to(v, (RET_DK, HEAD_W)), 0.0)
        nst_ref[0, h] = gam * state + _dot(k_first.T.astype(BF16), v_first.astype(BF16))
        o = o * lax.rsqrt(jnp.mean(o * o, axis=-1, keepdims=True) + EPS)
        ret_ref[0, :, hs] = o * (rg * jax.nn.sigmoid(rg))


def _retention_sample(rq, rk, rv, rg, state, gam):
    nb = rq.shape[0]
    tok = pl.BlockSpec((1, 1, GROUP_W), lambda b: (b, 0, 0))
    st = pl.BlockSpec((1, RET_HEADS, RET_DK, HEAD_W), lambda b: (b, 0, 0, 0))
    return pl.pallas_call(
        _retention_step_kernel,
        grid=(nb,),
        in_specs=[pl.BlockSpec(memory_space=pltpu.SMEM), tok, tok, tok, tok, st],
        out_specs=[tok, st],
        out_shape=[jax.ShapeDtypeStruct((nb, 1, GROUP_W), F32),
                   jax.ShapeDtypeStruct(state.shape, F32)],
        compiler_params=_params(("parallel",)),
        name="retention_sample",
    )(gam, rq, rk, rv, rg, state)


def _diff_prompt_kernel(qi_ref, kj_ref, sc_ref, q_ref, k_ref, v_ref, b_ref, o_ref,
                        q2_ref, m_ref, l_ref, acc_ref, *, out_scale):
    h = pl.program_id(1)
    step = pl.program_id(2)
    i = qi_ref[step]
    j = kj_ref[step]
    t = q_ref.shape[1]
    rows = 2 * t // ATTN_ROW_CHUNKS

    @pl.when(j == 0)
    def _():
        q = q_ref[0]
        lane = lax.broadcasted_iota(jnp.int32, q.shape, 1)
        zero = jnp.zeros_like(q)
        q2_ref[0:t, :] = jnp.where(lane < DIFF_DH, q, zero)
        q2_ref[t:2 * t, :] = jnp.where(lane < DIFF_DH, zero, q)
        m_ref[...] = jnp.full(m_ref.shape, NEG_INF, F32)
        l_ref[...] = jnp.zeros(l_ref.shape, F32)
        acc_ref[...] = jnp.zeros(acc_ref.shape, F32)

    def update(bias_of):
        k = k_ref[0]
        v = v_ref[0]
        for c in range(ATTN_ROW_CHUNKS):
            rs = slice(c * rows, (c + 1) * rows)
            s = _dot_nt(q2_ref[rs, :], k) + bias_of((c * rows) % t, rows)
            m_prev = m_ref[rs, :]
            m_new = jnp.maximum(m_prev, jnp.max(s, axis=-1, keepdims=True))
            alpha = jnp.exp(m_prev - m_new)
            p = jnp.exp(s - m_new)
            l_ref[rs, :] = alpha * l_ref[rs, :] + jnp.sum(p, axis=-1, keepdims=True)
            acc_ref[rs, :] = alpha * acc_ref[rs, :] + _dot(p.astype(BF16), v)
            m_ref[rs, :] = m_new

    d = i - j

    @pl.when(d >= 2)
    def _():
        far = sc_ref[h]
        update(lambda r0, n: far)

    @pl.when(d == 1)
    def _():
        update(lambda r0, n: b_ref[0, 1, r0:r0 + n, :])

    @pl.when(d == 0)
    def _():
        update(lambda r0, n: b_ref[0, 0, r0:r0 + n, :])
        lam = sc_ref[DIFF_HEADS]
        o = acc_ref[...] / l_ref[...]
        o = o[0:t] - lam * o[t:2 * t]
        o = o * lax.rsqrt(jnp.mean(o * o, axis=-1, keepdims=True) + EPS)
        o_ref[0] = (o * out_scale).astype(BF16)


def _diff_prompt(dqb, dkb, dvb, bias_tiles, scalars, out_scale):
    b, s, w = dqb.shape
    t = bias_tiles.shape[-1]
    n = s // t
    pairs = [(i, j) for i in range(n) for j in range(i + 1)]
    qi = jnp.asarray([p[0] for p in pairs], jnp.int32)
    kj = jnp.asarray([p[1] for p in pairs], jnp.int32)
    qspec = pl.BlockSpec((1, t, HEAD_W), lambda b_, h, st, qi_, kj_: (b_, qi_[st], h))
    kspec = pl.BlockSpec((1, t, HEAD_W), lambda b_, h, st, qi_, kj_: (b_, kj_[st], h))
    grid_spec = pltpu.PrefetchScalarGridSpec(
        num_scalar_prefetch=2,
        grid=(b, DIFF_HEADS, len(pairs)),
        in_specs=[pl.BlockSpec(memory_space=pltpu.SMEM), qspec, kspec, kspec,
                  pl.BlockSpec((1, 2, t, t), lambda b_, h, st, qi_, kj_: (h, 0, 0, 0))],
        out_specs=qspec,
        scratch_shapes=[pltpu.VMEM((2 * t, HEAD_W), BF16), pltpu.VMEM((2 * t, 1), F32),
                        pltpu.VMEM((2 * t, 1), F32), pltpu.VMEM((2 * t, HEAD_W), F32)],
    )
    return pl.pallas_call(
        functools.partial(_diff_prompt_kernel, out_scale=out_scale),
        grid_spec=grid_spec,
        out_shape=jax.ShapeDtypeStruct((b, s, w), BF16),
        compiler_params=_params(("parallel", "parallel", "arbitrary")),
        name="diff_attn_prompt",
    )(qi, kj, scalars, dqb, dkb, dvb, bias_tiles)


def _diff_sample_kernel(pt_ref, sc_ref, q_ref, kn_ref, vn_ref, bias_ref, *rest, n_pp, out_scale):
    k_refs = rest[:n_pp]
    v_refs = rest[n_pp:2 * n_pp]
    o_ref, qm_ref, m_ref, l_ref, acc_ref = rest[2 * n_pp:]
    j = pl.program_id(1)
    n_rows = 2 * DIFF_HEADS

    @pl.when(j == 0)
    def _():
        q = jnp.broadcast_to(q_ref[0], (n_rows, GROUP_W))
        lane = lax.broadcasted_iota(jnp.int32, (n_rows, GROUP_W), 1)
        row = lax.broadcasted_iota(jnp.int32, (n_rows, GROUP_W), 0)
        qm = jnp.where(lax.shift_right_logical(lane, int(math.log2(DIFF_DH))) == row, q, 0.0)
        qm_ref[...] = qm
        kn = kn_ref[0].astype(BF16).astype(F32)
        s_new = jnp.sum(qm * kn, axis=-1, keepdims=True)
        head = lax.shift_right_logical(lax.broadcasted_iota(jnp.int32, (n_rows, 1), 0), 1)
        b0 = jnp.zeros((n_rows, 1), F32)
        for hh in range(DIFF_HEADS):
            b0 = jnp.where(head == hh, sc_ref[DIFF_HEADS + 1 + hh], b0)
        m_ref[...] = s_new + b0
        l_ref[...] = jnp.ones(l_ref.shape, F32)
        acc_ref[...] = jnp.broadcast_to(vn_ref[0].astype(BF16).astype(F32), (n_rows, GROUP_W))

    qm = qm_ref[...].astype(BF16)
    s = jnp.concatenate([_dot(qm, k_refs[u][0].astype(BF16)) for u in range(n_pp)], axis=1)
    s = s + bias_ref[...]
    m_prev = m_ref[...]
    m_new = jnp.maximum(m_prev, jnp.max(s, axis=-1, keepdims=True))
    alpha = jnp.exp(m_prev - m_new)
    p = jnp.exp(s - m_new)
    l_ref[...] = alpha * l_ref[...] + jnp.sum(p, axis=-1, keepdims=True)
    pb = p.astype(BF16)
    for hh in range(DIFF_HEADS):
        hs = slice(hh * HEAD_W, (hh + 1) * HEAD_W)
        pv = None
        for u in range(n_pp):
            v_h = v_refs[u][0, pl.ds(hh, PAGE, stride=DIFF_HEADS), :].astype(BF16)
            d = _dot(pb[:, u * PAGE:(u + 1) * PAGE], v_h)
            pv = d if pv is None else pv + d
        acc_ref[:, hs] = alpha * acc_ref[:, hs] + pv
    m_ref[...] = m_new

    @pl.when(j == pl.num_programs(1) - 1)
    def _():
        lam = sc_ref[DIFF_HEADS]
        o = acc_ref[...] / l_ref[...]
        for hh in range(DIFF_HEADS):
            hs = slice(hh * HEAD_W, (hh + 1) * HEAD_W)
            oh = o[2 * hh:2 * hh + 1, hs] - lam * o[2 * hh + 1:2 * hh + 2, hs]
            oh = oh * lax.rsqrt(jnp.mean(oh * oh, axis=-1, keepdims=True) + EPS)
            o_ref[0, :, hs] = oh * out_scale


def _diff_sample(dqb, dk_new, dv_new, cache_k, cache_v, page_table, bias_past, scalars, out_scale):
    nb = dqb.shape[0]
    n_pages = page_table.shape[1]
    n_pp = PAGES_PER_STEP
    while n_pages % n_pp:
        n_pp //= 2
    tok = pl.BlockSpec((1, 1, GROUP_W), lambda b, j, pt: (b, 0, 0))

    def page_spec(u):
        return pl.BlockSpec((1, GROUP_W, PAGE), lambda b, j, pt: (pt[b, j * n_pp + u], 0, 0))

    n_rows = 2 * DIFF_HEADS
    grid_spec = pltpu.PrefetchScalarGridSpec(
        num_scalar_prefetch=1,
        grid=(nb, n_pages // n_pp),
        in_specs=[pl.BlockSpec(memory_space=pltpu.SMEM), tok, tok, tok,
                  pl.BlockSpec((n_rows, n_pp * PAGE), lambda b, j, pt: (0, j))]
                 + [page_spec(u) for u in range(n_pp)] * 2,
        out_specs=tok,
        scratch_shapes=[pltpu.VMEM((n_rows, GROUP_W), F32), pltpu.VMEM((n_rows, 1), F32),
                        pltpu.VMEM((n_rows, 1), F32), pltpu.VMEM((n_rows, GROUP_W), F32)],
    )
    return pl.pallas_call(
        functools.partial(_diff_sample_kernel, n_pp=n_pp, out_scale=out_scale),
        grid_spec=grid_spec,
        out_shape=jax.ShapeDtypeStruct((nb, 1, GROUP_W), F32),
        compiler_params=_params(("parallel", "arbitrary")),
        name="diff_attn_sample",
    )(page_table, scalars, dqb, dk_new, dv_new, bias_past,
      *([cache_k] * n_pp), *([cache_v] * n_pp))


def _mem_kv_kernel(m_ref, g_ref, wk_ref, wv_ref, k_ref, v_ref, kb_ref, vb_ref):
    mn = _rms(m_ref[...], g_ref[...]).astype(BF16)
    k = _dot(mn, wk_ref[...])
    v = _dot(mn, wv_ref[...])
    k_ref[...] = k
    v_ref[...] = v
    kb_ref[...] = k.astype(BF16)
    vb_ref[...] = v.astype(BF16)


def _mem_kv(mem, g, wk_bf, wv_bf, tm):
    n, d = mem.shape
    row = pl.BlockSpec((tm, d), lambda i: (i, 0))
    fix = lambda i: (0, 0)
    return pl.pallas_call(
        _mem_kv_kernel,
        grid=(n // tm,),
        in_specs=[row, pl.BlockSpec((1, d), fix), pl.BlockSpec((d, d), fix), pl.BlockSpec((d, d), fix)],
        out_specs=[row] * 4,
        out_shape=[jax.ShapeDtypeStruct((n, d), F32)] * 2 + [jax.ShapeDtypeStruct((n, d), BF16)] * 2,
        compiler_params=_params(("parallel",)),
        name="mem_kv",
    )(mem, g, wk_bf, wv_bf)


def _mix_residual(x, ret, dif, w_out_ref):
    half = w_out_ref.shape[0] // 2
    return x + _dot(ret.astype(BF16), w_out_ref[0:half, :]) + _dot(dif.astype(BF16), w_out_ref[half:, :])


def _cross_heads(q, mem_k, mem_v, o_ref):
    dh = q.shape[1] // MEM_HEADS
    for h in range(MEM_HEADS):
        hs = slice(h * dh, (h + 1) * dh)
        s = _dot_nt(q[:, hs], mem_k(h)) * (dh ** -0.5)
        e = jnp.exp(s - jnp.max(s, axis=-1, keepdims=True))
        p = e / jnp.sum(e, axis=-1, keepdims=True)
        o_ref[:, hs] = _dot(p.astype(BF16), mem_v(h)).astype(o_ref.dtype)


def _post_mix_kernel(x_ref, ret_ref, dif_ref, mk_ref, mv_ref, wo_ref, gc_ref, wq_ref, wm_ref, gf_ref,
                     h_ref, xt_ref, o_scr):
    h1 = _mix_residual(x_ref[...], ret_ref[...], dif_ref[...], wo_ref)
    q = _dot(_rms(h1, gc_ref[...]).astype(BF16), wq_ref[...]).astype(BF16)
    dh = q.shape[1] // MEM_HEADS
    _cross_heads(q, lambda h: mk_ref[0, :, h * dh:(h + 1) * dh], lambda h: mv_ref[0, :, h * dh:(h + 1) * dh],
                 o_scr)
    h2 = h1 + _dot(o_scr[...], wm_ref[...])
    h_ref[...] = h2
    xt_ref[...] = _rms(h2, gf_ref[...]).T.astype(BF16)


def _post_mix(x, ret, dif, mk_bf, mv_bf, wo_bf, g_cross, wq_bf, wm_bf, g_ffn, tm):
    n, d = x.shape
    nb, m, _ = mk_bf.shape
    per_batch = n // nb // tm
    row = lambda i: (i, 0)
    fix = lambda i: (0, 0)
    mem = pl.BlockSpec((1, m, d), lambda i: (i // per_batch, 0, 0))
    half = pl.BlockSpec((tm, GROUP_W), row)
    return pl.pallas_call(
        _post_mix_kernel,
        grid=(n // tm,),
        in_specs=[pl.BlockSpec((tm, d), row), half, half, mem, mem,
                  pl.BlockSpec((d, d), fix), pl.BlockSpec((1, d), fix), pl.BlockSpec((d, d), fix),
                  pl.BlockSpec((d, d), fix), pl.BlockSpec((1, d), fix)],
        out_specs=[pl.BlockSpec((tm, d), row), pl.BlockSpec((d, tm), lambda i: (0, i))],
        out_shape=[jax.ShapeDtypeStruct((n, d), F32), jax.ShapeDtypeStruct((d, n), BF16)],
        scratch_shapes=[pltpu.VMEM((tm, d), BF16)],
        compiler_params=_params(("parallel",)),
        name="post_mix",
    )(x, ret, dif, mk_bf, mv_bf, wo_bf, g_cross, wq_bf, wm_bf, g_ffn)


def _sample_mix_kernel(x_ref, ret_ref, dif_ref, wo_ref, gc_ref, wq_ref, h_ref, q_ref):
    h1 = _mix_residual(x_ref[...], ret_ref[...], dif_ref[...], wo_ref)
    h_ref[...] = h1
    q_ref[...] = _dot(_rms(h1, gc_ref[...]).astype(BF16), wq_ref[...])


def _sample_cross_kernel(q_ref, mk_ref, mv_ref, o_ref, o_scr):
    q8 = jnp.broadcast_to(q_ref[0], (8, q_ref.shape[2])).astype(BF16)
    dh = q8.shape[1] // MEM_HEADS
    _cross_heads(q8, lambda h: mk_ref[0, :, h * dh:(h + 1) * dh].astype(BF16),
                 lambda h: mv_ref[0, :, h * dh:(h + 1) * dh].astype(BF16), o_scr)
    o_ref[0] = o_scr[0:1, :]


def _sample_out_kernel(h_ref, o_ref, wm_ref, gf_ref, h2_ref, xt_ref, *, lanes):
    h2 = h_ref[...] + _dot(o_ref[...].astype(BF16), wm_ref[...])
    h2_ref[...] = h2
    xb = _rms(h2, gf_ref[...])
    pad = jnp.zeros((lanes - xb.shape[0], xb.shape[1]), F32)
    xt_ref[...] = jnp.concatenate([xb, pad], axis=0).T.astype(BF16)


def _post_mix_sample(x, ret, dif, mem_k, mem_v, wo_bf, g_cross, wq_bf, wm_bf, g_ffn, lanes):
    n, d = x.shape
    whole = lambda shape: pl.BlockSpec(shape, lambda: tuple(0 for _ in shape))
    h1, q = pl.pallas_call(
        _sample_mix_kernel,
        in_specs=[whole((n, d)), whole((n, GROUP_W)), whole((n, GROUP_W)), whole((d, d)),
                  whole((1, d)), whole((d, d))],
        out_specs=[whole((n, d)), whole((n, d))],
        out_shape=[jax.ShapeDtypeStruct((n, d), F32), jax.ShapeDtypeStruct((n, d), F32)],
        compiler_params=_params(()),
        name="sample_mix",
    )(x, ret, dif, wo_bf, g_cross, wq_bf)
    tok = pl.BlockSpec((1, 1, d), lambda b: (b, 0, 0))
    mem = pl.BlockSpec((1,) + mem_k.shape[1:], lambda b: (b, 0, 0))
    o = pl.pallas_call(
        _sample_cross_kernel,
        grid=(n,),
        in_specs=[tok, mem, mem],
        out_specs=tok,
        out_shape=jax.ShapeDtypeStruct((n, 1, d), F32),
        scratch_shapes=[pltpu.VMEM((8, d), F32)],
        compiler_params=_params(("parallel",)),
        name="sample_cross",
    )(q.reshape(n, 1, d), mem_k, mem_v)
    return pl.pallas_call(
        functools.partial(_sample_out_kernel, lanes=lanes),
        in_specs=[whole((n, d)), whole((n, d)), whole((d, d)), whole((1, d))],
        out_specs=[whole((n, d)), whole((d, lanes))],
        out_shape=[jax.ShapeDtypeStruct((n, d), F32), jax.ShapeDtypeStruct((d, lanes), BF16)],
        compiler_params=_params(()),
        name="sample_out",
    )(h1, o.reshape(n, d), wm_bf, g_ffn)


def _candidate_tiles():
    tiles = [("j", 0, 0, ()), ("j", 0, 8, ()), ("j", 1, 0, ())]
    tiles += [("i", 0, 0, (0, 1)), ("i", 0, 8, ())]
    tiles += [("i", j, 0, (0, 1)) for j in (1, 2, 3, 4)]
    return tiles


def _top_k_rows(s, k):
    n, t = s.shape
    rows = lax.broadcasted_iota(jnp.int32, s.shape, 0).astype(F32)
    sub = lax.broadcasted_iota(jnp.int32, (8, t), 0)
    rank = jnp.full(s.shape, float(k), F32)
    blocks = [jnp.zeros((8, t), F32) for _ in range(k // 8)]
    for it in range(k):
        m = jnp.max(s, axis=0, keepdims=True)
        first = jnp.min(jnp.where(s == m, rows, float(n)), axis=0, keepdims=True)
        sel = rows == first
        rank = jnp.where(sel, float(it), rank)
        s = jnp.where(sel, -jnp.inf, s)
        blocks[it // 8] = jnp.where(sub == it % 8, m, blocks[it // 8])
    return blocks, rank


def _route_lanes(s1, s2):
    k = PEER_TOPK
    t = s1.shape[1]
    tiles = _candidate_tiles()
    sub = lax.broadcasted_iota(jnp.int32, (8, t), 0)
    a, rank1 = _top_k_rows(s1, k)
    b, rank2 = _top_k_rows(s2, k)

    def row_of(blocks, i):
        return blocks[i // 8][i % 8:i % 8 + 1, :]

    cands, poss = [], []
    for kind, fixed, start, skip in tiles:
        if kind == "j":
            c = row_of(a, fixed) + b[start // 8]
            pos = fixed * k + start + sub
        else:
            c = a[start // 8] + row_of(b, fixed)
            pos = (start + sub) * k + fixed
            for r in skip:
                c = jnp.where(sub == r, -jnp.inf, c)
                pos = jnp.where(sub == r, k * k, pos)
        cands.append(c)
        poss.append(pos.astype(F32))
    c_all = jnp.concatenate(cands, axis=0)
    pos_all = jnp.concatenate(poss, axis=0)
    top = c_all[0:1, :]
    picked = jnp.zeros(c_all.shape, F32)
    work = c_all
    for _ in range(k):
        m = jnp.max(work, axis=0, keepdims=True)
        first = jnp.min(jnp.where(work == m, pos_all, float(k * k)), axis=0, keepdims=True)
        sel = pos_all == first
        picked = jnp.where(sel, 1.0, picked)
        work = jnp.where(sel, -jnp.inf, work)
    z = jnp.sum(picked * jnp.exp(jnp.where(picked > 0, c_all, top) - top), axis=0, keepdims=True)

    cnt_lo = jnp.zeros((8, t), F32)
    cnt_hi = jnp.zeros((8, t), F32)
    for n_tile, (kind, fixed, start, skip) in enumerate(tiles):
        pk = picked[n_tile * 8:(n_tile + 1) * 8, :]
        if kind == "j":
            tot = jnp.sum(pk, axis=0, keepdims=True)
            add = jnp.where(sub == fixed % 8, tot, 0.0)
            if fixed < 8:
                cnt_lo = cnt_lo + add
            else:
                cnt_hi = cnt_hi + add
        elif start == 0:
            cnt_lo = cnt_lo + pk
        else:
            cnt_hi = cnt_hi + pk
    cnt = [cnt_lo, cnt_hi]

    cnt1 = jnp.zeros(s1.shape, F32)
    for i in range(k):
        cnt1 = jnp.where(rank1 == float(i), row_of(cnt, i), cnt1)
    return rank2, jnp.exp(s2 - row_of(b, 0)), cnt1, jnp.exp(s1 - row_of(a, 0)) / z


def _route_kernel(xt_ref, wq_ref, keys_ref, rank2_ref, e2w_ref, cnt1_ref, e1w_ref, q_scr):
    q_scr[...] = _dot(wq_ref[...], xt_ref[...]).astype(BF16)
    t = xt_ref.shape[1]
    lane_w = 128

    def head(h, carry):
        base = pl.multiple_of(h * PEER_DQ, PEER_DQ)
        s1 = _dot(keys_ref[h, 0], q_scr[pl.ds(base, PEER_DQ // 2), :])
        s2 = _dot(keys_ref[h, 1], q_scr[pl.ds(base + PEER_DQ // 2, PEER_DQ // 2), :])
        for g in range(t // lane_w):
            ls = slice(g * lane_w, (g + 1) * lane_w)
            rank2, e2w, cnt1, e1w = _route_lanes(s1[:, ls], s2[:, ls])
            rank2_ref[h, :, ls] = rank2.astype(BF16)
            e2w_ref[h, :, ls] = e2w.astype(BF16)
            cnt1_ref[h, :, ls] = cnt1
            e1w_ref[h, :, ls] = e1w
        return carry

    lax.fori_loop(0, PEER_HEADS, head, 0)


def _route(xt, wq_t, keys_bf):
    d, n = xt.shape
    t = min(ROUTE_TILE, n)
    table = pl.BlockSpec((PEER_HEADS, N_KEYS, t), lambda i: (0, 0, i))
    shape = jax.ShapeDtypeStruct((PEER_HEADS, N_KEYS, n), F32)
    half_shape = jax.ShapeDtypeStruct((PEER_HEADS, N_KEYS, n), BF16)
    return pl.pallas_call(
        _route_kernel,
        grid=(n // t,),
        in_specs=[pl.BlockSpec((d, t), lambda i: (0, i)),
                  pl.BlockSpec(wq_t.shape, lambda i: (0, 0)),
                  pl.BlockSpec(keys_bf.shape, lambda i: (0, 0, 0, 0))],
        out_specs=[table] * 4,
        out_shape=[half_shape] * 2 + [shape] * 2,
        scratch_shapes=[pltpu.VMEM((PEER_HEADS * PEER_DQ, t), BF16)],
        compiler_params=_params(("parallel",)),
        name="peer_route",
    )(xt, wq_t, keys_bf)


def _peer_gate(a_ref, hid_ref, step, odd, rank2_ref, e2w_ref, cnt1_ref, e1w_ref):
    te, tm = a_ref.shape
    lane_w = 128
    keys_per_tile = te // N_KEYS
    assert (2 * keys_per_tile) % 8 == 0
    zero = jnp.zeros((N_KEYS, lane_w), BF16)
    for r in range(keys_per_tile):
        rs = slice(r * N_KEYS, (r + 1) * N_KEYS)
        e1 = odd * keys_per_tile + r
        e1_blk = pl.multiple_of(step * 2 * keys_per_tile + 8 * (e1 // 8), 8)
        for c in range(tm // lane_w):
            ls = slice(c * lane_w, (c + 1) * lane_w)
            gate = zero
            for h in range(PEER_HEADS):
                cnt = cnt1_ref[h, pl.ds(e1_blk, 8), ls][e1 % 8:e1 % 8 + 1, :]
                w1 = e1w_ref[h, pl.ds(e1_blk, 8), ls][e1 % 8:e1 % 8 + 1, :]
                cnt = jnp.broadcast_to(cnt, (N_KEYS, lane_w)).astype(BF16)
                w1 = jnp.broadcast_to(w1, (N_KEYS, lane_w)).astype(BF16)
                gate = gate + jnp.where(rank2_ref[h, :, ls] < cnt, e2w_ref[h, :, ls], zero) * w1
            a = a_ref[rs, ls]
            act = 0.5 * a * (1.0 + lax.erf(a * (2.0 ** -0.5)))
            hid_ref[rs, ls] = act.astype(BF16) * gate


def _peer_kernel(xt_ref, u_first_ref, u_odd_ref, u_next_ref, vt_prev_ref, vt_even_ref, vt_last_ref,
                 rank2_in_ref, e2w_in_ref, cnt1_ref, e1w_ref, h_ref, g_ref,
                 y_ref, acc_ref, a0_ref, a1_ref, hid0_ref, hid1_ref, rank2_ref, e2w_ref):
    s = pl.program_id(1)
    tables = (rank2_ref, e2w_ref, cnt1_ref, e1w_ref)
    xt = xt_ref[...]

    @pl.when(s == 0)
    def _():
        acc_ref[...] = jnp.zeros(acc_ref.shape, F32)
        hid1_ref[...] = jnp.zeros(hid1_ref.shape, BF16)
        a0_ref[...] = _dot(u_first_ref[...], xt)
        rank2_ref[...] = rank2_in_ref[...]
        e2w_ref[...] = e2w_in_ref[...]

    _peer_gate(a0_ref, hid0_ref, s, 0, *tables)
    out = _dot(vt_prev_ref[0], hid1_ref[...])
    a1_ref[...] = _dot(u_odd_ref[...], xt)
    _peer_gate(a1_ref, hid1_ref, s, 1, *tables)
    out = out + _dot(vt_even_ref[0], hid0_ref[...])
    a0_ref[...] = _dot(u_next_ref[...], xt)
    acc_ref[...] += out

    @pl.when(s == pl.num_programs(1) - 1)
    def _():
        acc = acc_ref[...] + _dot(vt_last_ref[0], hid1_ref[...])
        y_ref[...] = _rms(h_ref[...] + acc.T, g_ref[...])


def _peer(xt, u_bf, vt_tiles, tables, h2, g_final, tm):
    d, n = xt.shape
    n_tiles, _, te = vt_tiles.shape
    n_steps = n_tiles // 2
    assert n_tiles == 2 * n_steps
    once = pl.Buffered(1)
    table = pl.BlockSpec((PEER_HEADS, N_KEYS, tm), lambda i, s: (0, 0, i), pipeline_mode=once)
    u_spec = lambda tile, mode=None: pl.BlockSpec((te, d), lambda i, s: (tile(s), 0), pipeline_mode=mode)
    vt_spec = lambda tile, mode=None: pl.BlockSpec((1, d, te), lambda i, s: (tile(s), 0, 0),
                                                   pipeline_mode=mode)
    return pl.pallas_call(
        _peer_kernel,
        grid=(n // tm, n_steps),
        in_specs=[pl.BlockSpec((d, tm), lambda i, s: (0, i), pipeline_mode=once),
                  u_spec(lambda s: 0, once),
                  u_spec(lambda s: 2 * s + 1),
                  u_spec(lambda s: jnp.minimum(2 * s + 2, n_tiles - 1)),
                  vt_spec(lambda s: jnp.maximum(2 * s - 1, 0)),
                  vt_spec(lambda s: 2 * s),
                  vt_spec(lambda s: n_tiles - 1, once),
                  table, table, table, table,
                  pl.BlockSpec((tm, d), lambda i, s: (i, 0), pipeline_mode=once),
                  pl.BlockSpec((1, d), lambda i, s: (0, 0))],
        out_specs=pl.BlockSpec((tm, d), lambda i, s: (i, 0)),
        out_shape=jax.ShapeDtypeStruct((n, d), F32),
        scratch_shapes=[pltpu.VMEM((d, tm), F32), pltpu.VMEM((te, tm), F32), pltpu.VMEM((te, tm), F32),
                        pltpu.VMEM((te, tm), BF16), pltpu.VMEM((te, tm), BF16),
                        pltpu.VMEM((PEER_HEADS, N_KEYS, tm), BF16), pltpu.VMEM((PEER_HEADS, N_KEYS, tm), BF16)],
        compiler_params=_params(("parallel", "arbitrary"), PEER_VMEM_LIMIT),
        name="peer_dense",
    )(xt, u_bf, u_bf, u_bf, vt_tiles, vt_tiles, vt_tiles, *tables, h2, g_final)


def _t5_bias(rel, table):
    n = jnp.maximum(rel, 0)
    max_exact = N_BUCKETS // 2
    nf = jnp.maximum(n, 1).astype(F32)
    large = max_exact + (jnp.log(nf / max_exact) / math.log(MAX_DISTANCE / max_exact)
                         * (N_BUCKETS - max_exact)).astype(jnp.int32)
    large = jnp.minimum(large, N_BUCKETS - 1)
    bucket = jnp.where(n < max_exact, n, large)
    return jnp.moveaxis(table[bucket].astype(F32), -1, 0)


def _toeplitz(vec, t):
    hh = vec.shape[0]
    flat = jnp.tile(vec, (1, t))[:, :t * (2 * t - 1)]
    return flat.reshape(hh, t, 2 * t - 1)[:, :, :t]


def _prompt_bias_tiles(rel_bias, t):
    k = jnp.arange(2 * t)
    tiles = []
    for d in (0, 1):
        rel = jnp.where(k < t, d * t - k, d * t + 2 * t - k)
        vec = _t5_bias(rel, rel_bias)
        vec = jnp.where(rel[None, :] >= 0, vec, NEG_INF)
        tiles.append(_toeplitz(vec, t))
    return jnp.stack(tiles, axis=1)


def _rotary_tables(pos):
    half = RET_DK // 2
    inv = 1.0 / (ROPE_BASE ** (jnp.arange(half, dtype=F32) / half))
    ang = pos.astype(F32)[:, None] * inv[None, :]
    cos, sin = jnp.cos(ang), jnp.sin(ang)
    return jnp.concatenate([cos, cos], axis=1), jnp.concatenate([-sin, sin], axis=1)


def _retention_tables(log_gamma):
    n = jnp.arange(RET_CHUNK, dtype=F32)
    diff = n[:, None] - n[None, :]
    decay = jnp.where(diff[None] >= 0,
                      jnp.exp(jnp.maximum(diff, 0.0)[None] * log_gamma[:, None, None]), 0.0)
    ones = jnp.ones((1, 1, HEAD_W), F32)
    qdec = jnp.exp((n + 1.0)[None, :] * log_gamma[:, None])[:, :, None] * ones
    kdec = jnp.exp((RET_CHUNK - 1.0 - n)[None, :] * log_gamma[:, None])[:, :, None] * ones
    return decay, qdec, kdec, jnp.exp(RET_CHUNK * log_gamma)


def kernel(x_prompt, x_sample, mem_prompt, cache_diff_k, cache_diff_v, page_table, state_ret,
           cache_mem_k, cache_mem_v, g_mix, w_in, w_out, lam_q1, lam_k1, lam_q2, lam_k2, rel_bias,
           g_cross, g_mem, w_mq, w_mk, w_mv, w_mo, g_ffn, w_pq, sub_keys, peer_u, peer_v, g_final):
    depth = g_mix.shape[0]
    assert depth == 1
    l = 0
    b, s, d = x_prompt.shape
    db, t_s, _ = x_sample.shape
    assert t_s == 1
    n_pages = page_table.shape[1]
    past = n_pages * PAGE
    mem_len = mem_prompt.shape[1]
    n_tok = b * s

    log_gamma = jnp.log(1.0 - 2.0 ** (-5.0 - jnp.arange(RET_HEADS, dtype=F32)))
    lam_init = 0.8 - 0.6 * math.exp(-0.3 * l)
    lam = (jnp.exp(jnp.sum(lam_q1[l].astype(F32) * lam_k1[l].astype(F32)))
           - jnp.exp(jnp.sum(lam_q2[l].astype(F32) * lam_k2[l].astype(F32))) + lam_init)
    out_scale = 1.0 - lam_init

    row = lambda v: v.reshape(1, -1)
    w_in_bf = w_in[l].astype(BF16)
    w_out_bf = w_out[l].astype(BF16)
    w_mq_bf = w_mq[l].astype(BF16)
    w_mk_bf = w_mk[l].astype(BF16)
    w_mv_bf = w_mv[l].astype(BF16)
    w_mo_bf = w_mo[l].astype(BF16)
    w_pq_t = w_pq[l].T.astype(BF16)
    keys_bf = sub_keys[l].astype(BF16)
    u_bf = peer_u[l].astype(BF16)
    n_experts = peer_v.shape[1]
    vt_bf = jnp.transpose(peer_v[l].astype(BF16).reshape(n_experts // PEER_EXPERT_TILE, PEER_EXPERT_TILE, d),
                          (0, 2, 1))
    g_fin = row(g_final)

    bias_far = _t5_bias(jnp.full((1,), MAX_DISTANCE, jnp.int32), rel_bias)[:, 0]
    bias_zero = _t5_bias(jnp.zeros((1,), jnp.int32), rel_bias)[:, 0]
    scalars = jnp.concatenate([bias_far, lam.reshape(1), bias_zero]).astype(F32)

    cs_p, sn_p = _rotary_tables(jnp.arange(s))
    rq, rk, rv, rg, dqb, dk, dv, dkb, dvb = _in_proj(
        x_prompt.reshape(n_tok, d), row(g_mix[l]), w_in_bf, cs_p, sn_p, TOKEN_TILE)
    decay, qdec, kdec, gl = _retention_tables(log_gamma)
    shp = (b, s, GROUP_W)
    ret_p, state_p = _retention_prompt(rq.reshape(shp), rk.reshape(shp), rv.reshape(shp),
                                       rg.reshape(shp), decay, qdec, kdec, gl)
    dif_p = _diff_prompt(dqb.reshape(shp), dkb.reshape(shp), dvb.reshape(shp),
                         _prompt_bias_tiles(rel_bias, ATTN_TILE), scalars, out_scale)
    mk, mv, mk_bf, mv_bf = _mem_kv(mem_prompt.reshape(b * mem_len, d), row(g_mem[l]),
                                   w_mk_bf, w_mv_bf, mem_len)
    h2_p, xt_p = _post_mix(x_prompt.reshape(n_tok, d), ret_p.reshape(n_tok, GROUP_W),
                           dif_p.reshape(n_tok, GROUP_W), mk_bf.reshape(b, mem_len, d),
                           mv_bf.reshape(b, mem_len, d), w_out_bf, row(g_cross[l]), w_mq_bf,
                           w_mo_bf, row(g_ffn[l]), TOKEN_TILE)
    tables_p = _route(xt_p, w_pq_t, keys_bf)
    y_p = _peer(xt_p, u_bf, vt_bf, tables_p, h2_p, g_fin, PEER_TOKEN_TILE)

    lanes = SAMPLE_LANES
    assert db <= lanes
    cs_s, sn_s = _rotary_tables(jnp.full((db,), past, jnp.int32))
    xs = x_sample.reshape(db, d)
    rq_s, rk_s, rv_s, rg_s, dqb_s, dk_s, dv_s, _, _ = _in_proj(
        xs, row(g_mix[l]), w_in_bf, cs_s, sn_s, db)
    tok = (db, 1, GROUP_W)
    ret_s, state_s = _retention_sample(rq_s.reshape(tok), rk_s.reshape(tok), rv_s.reshape(tok),
                                       rg_s.reshape(tok), state_ret[l], jnp.exp(log_gamma))
    near = _t5_bias(jnp.arange(MAX_DISTANCE, 0, -1), rel_bias)
    far = jnp.broadcast_to(bias_far[:, None], (DIFF_HEADS, past - MAX_DISTANCE))
    bias_past = jnp.repeat(jnp.concatenate([far, near], axis=1), 2, axis=0)
    dif_s = _diff_sample(dqb_s.astype(F32).reshape(tok), dk_s.reshape(tok), dv_s.reshape(tok),
                         jnp.transpose(cache_diff_k[l], (0, 2, 3, 4, 1)).reshape(-1, GROUP_W, PAGE),
                         cache_diff_v[l].reshape(-1, PAGE * DIFF_HEADS, HEAD_W),
                         page_table, bias_past, scalars, out_scale)
    h2_s, xt_s = _post_mix_sample(xs, ret_s.reshape(db, GROUP_W), dif_s.reshape(db, GROUP_W),
                                  cache_mem_k[l].reshape(db, mem_len, d),
                                  cache_mem_v[l].reshape(db, mem_len, d),
                                  w_out_bf, row(g_cross[l]), w_mq_bf, w_mo_bf, row(g_ffn[l]), lanes)
    tables_s = _route(xt_s, w_pq_t, keys_bf)
    h2_pad = jnp.concatenate([h2_s, jnp.zeros((lanes - db, d), F32)], axis=0)
    y_s = _peer(xt_s, u_bf, vt_bf, tables_s, h2_pad, g_fin, lanes)[:db]

    return (y_p.reshape(b, s, d), y_s.reshape(db, 1, d),
            dk.reshape(1, b, s, DIFF_HEADS, 2, DIFF_DH), dv.reshape(1, b, s, DIFF_HEADS, HEAD_W),
            state_p[None], mk.reshape(1, b, mem_len, MEM_HEADS, d // MEM_HEADS),
            mv.reshape(1, b, mem_len, MEM_HEADS, d // MEM_HEADS),
            dk_s.reshape(1, db, 1, DIFF_HEADS, 2, DIFF_DH), dv_s.reshape(1, db, 1, DIFF_HEADS, HEAD_W),
            state_s[None])
```

```python
import functools
import math

import jax
import jax.numpy as jnp
import numpy as np
from jax import lax
from jax.experimental import pallas as pl
from jax.experimental.pallas import tpu as pltpu

F32 = jnp.float32
BF16 = jnp.bfloat16

EPS = 1e-6
NEG_INF = -1e30
ROPE_BASE = 10000.0
RET_HEADS = 4
RET_DK = 128
RET_CHUNK = 128
DIFF_HEADS = 4
DIFF_DH = 64
HEAD_W = 128
GROUP_W = 512
N_GROUPS = 7
PAGE = 128
N_BUCKETS = 32
MAX_DISTANCE = 128
MEM_HEADS = 4
PEER_HEADS = 8
N_KEYS = 128
PEER_TOPK = 16
PEER_DQ = 256

VMEM_LIMIT = 56 * 1024 * 1024
ATTN_TILE = 512
ATTN_ROW_CHUNKS = 4
TOKEN_TILE = 512
PEER_EXPERT_TILE = 1024
PEER_TOKEN_TILE = 512
PEER_VMEM_LIMIT = 60 * 1024 * 1024
ROUTE_TILE = 256
SAMPLE_LANES = 128
PAGES_PER_STEP = 8


def _params(semantics, vmem=VMEM_LIMIT):
    return pltpu.CompilerParams(dimension_semantics=semantics, vmem_limit_bytes=vmem)


def _rms(x, g):
    return (x * lax.rsqrt(jnp.mean(x * x, axis=-1, keepdims=True) + EPS)) * g


def _dot(a, b):
    return jnp.dot(a, b, preferred_element_type=F32)


def _dot_nt(a, b):
    return lax.dot_general(a, b, (((1,), (1,)), ((), ())), preferred_element_type=F32)


def _in_proj_kernel(x_ref, g_ref, w_ref, cs_ref, sn_ref,
                    rq_ref, rk_ref, rv_ref, rg_ref, dqb_ref, dk_ref, dv_ref, dkb_ref, dvb_ref):
    xb = _rms(x_ref[...], g_ref[...]).astype(BF16)
    cs = cs_ref[...]
    sn = sn_ref[...]

    def proj(c):
        return _dot(xb, w_ref[:, c * GROUP_W:(c + 1) * GROUP_W])

    def rotary(p, out_ref, scale):
        for h in range(RET_HEADS):
            ph = p[:, h * HEAD_W:(h + 1) * HEAD_W]
            r = ph * cs + pltpu.roll(ph, HEAD_W // 2, 1) * sn
            if scale is not None:
                r = r * scale
            out_ref[:, h * HEAD_W:(h + 1) * HEAD_W] = r

    rotary(proj(0), rq_ref, RET_DK ** -0.5)
    rotary(proj(1), rk_ref, None)
    rv_ref[...] = proj(2)
    rg_ref[...] = proj(3)
    dqb_ref[...] = (proj(4) * (DIFF_DH ** -0.5)).astype(BF16)
    dk = proj(5)
    dk_ref[...] = dk
    dkb_ref[...] = dk.astype(BF16)
    dv = proj(6)
    dv_ref[...] = dv
    dvb_ref[...] = dv.astype(BF16)


def _in_proj(x, g, w_bf, cs, sn, tm):
    n, d = x.shape
    nt = n // tm
    n_pos = cs.shape[0] // tm
    row = lambda i: (i, 0)
    pos = lambda i: (i % n_pos, 0)
    fix = lambda i: (0, 0)
    wide = pl.BlockSpec((tm, GROUP_W), row)
    f32o = jax.ShapeDtypeStruct((n, GROUP_W), F32)
    bf16o = jax.ShapeDtypeStruct((n, GROUP_W), BF16)
    return pl.pallas_call(
        _in_proj_kernel,
        grid=(nt,),
        in_specs=[pl.BlockSpec((tm, d), row), pl.BlockSpec((1, d), fix),
                  pl.BlockSpec((d, N_GROUPS * GROUP_W), fix),
                  pl.BlockSpec((tm, HEAD_W), pos), pl.BlockSpec((tm, HEAD_W), pos)],
        out_specs=[wide] * 9,
        out_shape=[f32o, f32o, f32o, f32o, bf16o, f32o, f32o, bf16o, bf16o],
        compiler_params=_params(("parallel",)),
        name="in_proj",
    )(x, g, w_bf, cs, sn)


def _retention_kernel(gl_ref, rq_ref, rk_ref, rv_ref, rg_ref, dec_ref, qd_ref, kd_ref,
                      ret_ref, st_ref):
    c = pl.program_id(0)

    @pl.when(c == 0)
    def _():
        st_ref[...] = jnp.zeros(st_ref.shape, F32)

    nb = rq_ref.shape[0]
    for b in range(nb):
        for h in range(RET_HEADS):
            hs = slice(h * HEAD_W, (h + 1) * HEAD_W)
            q = rq_ref[b, :, hs].astype(BF16)
            k = rk_ref[b, :, hs]
            v = rv_ref[b, :, hs].astype(BF16)
            state = st_ref[b, h]
            scores = _dot_nt(q, k.astype(BF16)) * dec_ref[h]
            inner = _dot(scores.astype(BF16), v)
            cross = _dot(q, state.astype(BF16)) * qd_ref[h]
            o = inner + cross
            kd = (k * kd_ref[h]).T.astype(BF16)
            st_ref[b, h] = gl_ref[h] * state + _dot(kd, v)
            rg = rg_ref[b, :, hs]
            o = o * lax.rsqrt(jnp.mean(o * o, axis=-1, keepdims=True) + EPS)
            ret_ref[b, :, hs] = (o * (rg * jax.nn.sigmoid(rg))).astype(BF16)


def _retention_prompt(rq, rk, rv, rg, decay, qdec, kdec, gl):
    b, s, w = rq.shape
    nc = s // RET_CHUNK
    tok = pl.BlockSpec((b, RET_CHUNK, w), lambda c: (0, c, 0))
    const = pl.BlockSpec((RET_HEADS, RET_CHUNK, RET_CHUNK), lambda c: (0, 0, 0))
    return pl.pallas_call(
        _retention_kernel,
        grid=(nc,),
        in_specs=[pl.BlockSpec(memory_space=pltpu.SMEM), tok, tok, tok, tok, const, const, const],
        out_specs=[tok, pl.BlockSpec((b, RET_HEADS, RET_DK, HEAD_W), lambda c: (0, 0, 0, 0))],
        out_shape=[jax.ShapeDtypeStruct((b, s, w), BF16),
                   jax.ShapeDtypeStruct((b, RET_HEADS, RET_DK, HEAD_W), F32)],
        compiler_params=_params(("arbitrary",)),
        name="retention_prompt",
    )(gl, rq, rk, rv, rg, decay, qdec, kdec)


def _retention_step_kernel(gam_ref, rq_ref, rk_ref, rv_ref, rg_ref, st_ref, ret_ref, nst_ref):
    rows = lax.broadcasted_iota(jnp.int32, (RET_DK, HEAD_W), 0)
    for h in range(RET_HEADS):
        hs = slice(h * HEAD_W, (h + 1) * HEAD_W)
        q = rq_ref[0, :, hs]
        k = rk_ref[0, :, hs]
        v = rv_ref[0, :, hs]
        rg = rg_ref[0, :, hs]
        state = st_ref[0, h]
        gam = gam_ref[h]
        q8 = jnp.broadcast_to(q, (8, HEAD_W)).astype(BF16)
        qk = jnp.sum(q.astype(BF16).astype(F32) * k.astype(BF16).astype(F32), axis=-1, keepdims=True)
        cross = _dot(q8, state.astype(BF16))[0:1, :] * gam
        o = qk * v.astype(BF16).astype(F32) + cross
        k_first = jnp.where(rows == 0, jnp.broadcast_to(k, (RET_DK, HEAD_W)), 0.0)
        v_first = jnp.where(rows == 0, jnp.broadcast_to(v, (RET_DK, HEAD_W)), 0.0)
        nst_ref[0, h] = gam * state + _dot(k_first.T.astype(BF16), v_first.astype(BF16))
        o = o * lax.rsqrt(jnp.mean(o * o, axis=-1, keepdims=True) + EPS)
        ret_ref[0, :, hs] = o * (rg * jax.nn.sigmoid(rg))


def _retention_sample(rq, rk, rv, rg, state, gam):
    nb = rq.shape[0]
    tok = pl.BlockSpec((1, 1, GROUP_W), lambda b: (b, 0, 0))
    st = pl.BlockSpec((1, RET_HEADS, RET_DK, HEAD_W), lambda b: (b, 0, 0, 0))
    return pl.pallas_call(
        _retention_step_kernel,
        grid=(nb,),
        in_specs=[pl.BlockSpec(memory_space=pltpu.SMEM), tok, tok, tok, tok, st],
        out_specs=[tok, st],
        out_shape=[jax.ShapeDtypeStruct((nb, 1, GROUP_W), F32),
                   jax.ShapeDtypeStruct(state.shape, F32)],
        compiler_params=_params(("parallel",)),
        name="retention_sample",
    )(gam, rq, rk, rv, rg, state)


def _diff_prompt_kernel(qi_ref, kj_ref, sc_ref, q_ref, k_ref, v_ref, b_ref, o_ref,
                        q2_ref, m_ref, l_ref, acc_ref, *, out_scale):
    h = pl.program_id(1)
    step = pl.program_id(2)
    i = qi_ref[step]
    j = kj_ref[step]
    t = q_ref.shape[1]
    rows = 2 * t // ATTN_ROW_CHUNKS

    @pl.when(j == 0)
    def _():
        q = q_ref[0]
        lane = lax.broadcasted_iota(jnp.int32, q.shape, 1)
        zero = jnp.zeros_like(q)
        q2_ref[0:t, :] = jnp.where(lane < DIFF_DH, q, zero)
        q2_ref[t:2 * t, :] = jnp.where(lane < DIFF_DH, zero, q)
        m_ref[...] = jnp.full(m_ref.shape, NEG_INF, F32)
        l_ref[...] = jnp.zeros(l_ref.shape, F32)
        acc_ref[...] = jnp.zeros(acc_ref.shape, F32)

    def update(bias_of):
        k = k_ref[0]
        v = v_ref[0]
        for c in range(ATTN_ROW_CHUNKS):
            rs = slice(c * rows, (c + 1) * rows)
            s = _dot_nt(q2_ref[rs, :], k) + bias_of((c * rows) % t, rows)
            m_prev = m_ref[rs, :]
            m_new = jnp.maximum(m_prev, jnp.max(s, axis=-1, keepdims=True))
            alpha = jnp.exp(m_prev - m_new)
            p = jnp.exp(s - m_new)
            l_ref[rs, :] = alpha * l_ref[rs, :] + jnp.sum(p, axis=-1, keepdims=True)
            acc_ref[rs, :] = alpha * acc_ref[rs, :] + _dot(p.astype(BF16), v)
            m_ref[rs, :] = m_new

    d = i - j

    @pl.when(d >= 2)
    def _():
        far = sc_ref[h]
        update(lambda r0, n: far)

    @pl.when(d == 1)
    def _():
        update(lambda r0, n: b_ref[0, 1, r0:r0 + n, :])

    @pl.when(d == 0)
    def _():
        update(lambda r0, n: b_ref[0, 0, r0:r0 + n, :])
        lam = sc_ref[DIFF_HEADS]
        o = acc_ref[...] / l_ref[...]
        o = o[0:t] - lam * o[t:2 * t]
        o = o * lax.rsqrt(jnp.mean(o * o, axis=-1, keepdims=True) + EPS)
        o_ref[0] = (o * out_scale).astype(BF16)


def _diff_prompt(dqb, dkb, dvb, bias_tiles, scalars, out_scale):
    b, s, w = dqb.shape
    t = bias_tiles.shape[-1]
    n = s // t
    pairs = [(i, j) for i in range(n) for j in range(i + 1)]
    qi = jnp.asarray([p[0] for p in pairs], jnp.int32)
    kj = jnp.asarray([p[1] for p in pairs], jnp.int32)
    qspec = pl.BlockSpec((1, t, HEAD_W), lambda b_, h, st, qi_, kj_: (b_, qi_[st], h))
    kspec = pl.BlockSpec((1, t, HEAD_W), lambda b_, h, st, qi_, kj_: (b_, kj_[st], h))
    grid_spec = pltpu.PrefetchScalarGridSpec(
        num_scalar_prefetch=2,
        grid=(b, DIFF_HEADS, len(pairs)),
        in_specs=[pl.BlockSpec(memory_space=pltpu.SMEM), qspec, kspec, kspec,
                  pl.BlockSpec((1, 2, t, t), lambda b_, h, st, qi_, kj_: (h, 0, 0, 0))],
        out_specs=qspec,
        scratch_shapes=[pltpu.VMEM((2 * t, HEAD_W), BF16), pltpu.VMEM((2 * t, 1), F32),
                        pltpu.VMEM((2 * t, 1), F32), pltpu.VMEM((2 * t, HEAD_W), F32)],
    )
    return pl.pallas_call(
        functools.partial(_diff_prompt_kernel, out_scale=out_scale),
        grid_spec=grid_spec,
        out_shape=jax.ShapeDtypeStruct((b, s, w), BF16),
        compiler_params=_params(("parallel", "parallel", "arbitrary")),
        name="diff_attn_prompt",
    )(qi, kj, scalars, dqb, dkb, dvb, bias_tiles)


def _diff_sample_kernel(pt_ref, sc_ref, q_ref, kn_ref, vn_ref, bias_ref, *rest, n_pp, out_scale):
    k_refs = rest[:n_pp]
    v_refs = rest[n_pp:2 * n_pp]
    o_ref, qm_ref, m_ref, l_ref, acc_ref = rest[2 * n_pp:]
    j = pl.program_id(1)
    n_rows = 2 * DIFF_HEADS

    @pl.when(j == 0)
    def _():
        q = jnp.broadcast_to(q_ref[0], (n_rows, GROUP_W))
        lane = lax.broadcasted_iota(jnp.int32, (n_rows, GROUP_W), 1)
        row = lax.broadcasted_iota(jnp.int32, (n_rows, GROUP_W), 0)
        qm = jnp.where(lax.shift_right_logical(lane, int(math.log2(DIFF_DH))) == row, q, 0.0)
        qm_ref[...] = qm
        kn = kn_ref[0].astype(BF16).astype(F32)
        s_new = jnp.sum(qm * kn, axis=-1, keepdims=True)
        head = lax.shift_right_logical(lax.broadcasted_iota(jnp.int32, (n_rows, 1), 0), 1)
        b0 = jnp.zeros((n_rows, 1), F32)
        for hh in range(DIFF_HEADS):
            b0 = jnp.where(head == hh, sc_ref[DIFF_HEADS + 1 + hh], b0)
        m_ref[...] = s_new + b0
        l_ref[...] = jnp.ones(l_ref.shape, F32)
        acc_ref[...] = jnp.broadcast_to(vn_ref[0].astype(BF16).astype(F32), (n_rows, GROUP_W))

    qm = qm_ref[...].astype(BF16)
    s = jnp.concatenate([_dot(qm, k_refs[u][0].astype(BF16)) for u in range(n_pp)], axis=1)
    s = s + bias_ref[...]
    m_prev = m_ref[...]
    m_new = jnp.maximum(m_prev, jnp.max(s, axis=-1, keepdims=True))
    alpha = jnp.exp(m_prev - m_new)
    p = jnp.exp(s - m_new)
    l_ref[...] = alpha * l_ref[...] + jnp.sum(p, axis=-1, keepdims=True)
    pb = p.astype(BF16)
    for hh in range(DIFF_HEADS):
        hs = slice(hh * HEAD_W, (hh + 1) * HEAD_W)
        pv = None
        for u in range(n_pp):
            v_h = v_refs[u][0, pl.ds(hh, PAGE, stride=DIFF_HEADS), :].astype(BF16)
            d = _dot(pb[:, u * PAGE:(u + 1) * PAGE], v_h)
            pv = d if pv is None else pv + d
        acc_ref[:, hs] = alpha * acc_ref[:, hs] + pv
    m_ref[...] = m_new

    @pl.when(j == pl.num_programs(1) - 1)
    def _():
        lam = sc_ref[DIFF_HEADS]
        o = acc_ref[...] / l_ref[...]
        for hh in range(DIFF_HEADS):
            hs = slice(hh * HEAD_W, (hh + 1) * HEAD_W)
            oh = o[2 * hh:2 * hh + 1, hs] - lam * o[2 * hh + 1:2 * hh + 2, hs]
            oh = oh * lax.rsqrt(jnp.mean(oh * oh, axis=-1, keepdims=True) + EPS)
            o_ref[0, :, hs] = oh * out_scale


def _diff_sample(dqb, dk_new, dv_new, cache_k, cache_v, page_table, bias_past, scalars, out_scale):
    nb = dqb.shape[0]
    n_pages = page_table.shape[1]
    n_pp = PAGES_PER_STEP
    while n_pages % n_pp:
        n_pp //= 2
    tok = pl.BlockSpec((1, 1, GROUP_W), lambda b, j, pt: (b, 0, 0))

    def page_spec(u):
        return pl.BlockSpec((1, GROUP_W, PAGE), lambda b, j, pt: (pt[b, j * n_pp + u], 0, 0))

    n_rows = 2 * DIFF_HEADS
    grid_spec = pltpu.PrefetchScalarGridSpec(
        num_scalar_prefetch=1,
        grid=(nb, n_pages // n_pp),
        in_specs=[pl.BlockSpec(memory_space=pltpu.SMEM), tok, tok, tok,
                  pl.BlockSpec((n_rows, n_pp * PAGE), lambda b, j, pt: (0, j))]
                 + [page_spec(u) for u in range(n_pp)] * 2,
        out_specs=tok,
        scratch_shapes=[pltpu.VMEM((n_rows, GROUP_W), F32), pltpu.VMEM((n_rows, 1), F32),
                        pltpu.VMEM((n_rows, 1), F32), pltpu.VMEM((n_rows, GROUP_W), F32)],
    )
    return pl.pallas_call(
        functools.partial(_diff_sample_kernel, n_pp=n_pp, out_scale=out_scale),
        grid_spec=grid_spec,
        out_shape=jax.ShapeDtypeStruct((nb, 1, GROUP_W), F32),
        compiler_params=_params(("parallel", "arbitrary")),
        name="diff_attn_sample",
    )(page_table, scalars, dqb, dk_new, dv_new, bias_past,
      *([cache_k] * n_pp), *([cache_v] * n_pp))


def _mem_kv_kernel(m_ref, g_ref, wk_ref, wv_ref, k_ref, v_ref, kb_ref, vb_ref):
    mn = _rms(m_ref[...], g_ref[...]).astype(BF16)
    k = _dot(mn, wk_ref[...])
    v = _dot(mn, wv_ref[...])
    k_ref[...] = k
    v_ref[...] = v
    kb_ref[...] = k.astype(BF16)
    vb_ref[...] = v.astype(BF16)


def _mem_kv(mem, g, wk_bf, wv_bf, tm):
    n, d = mem.shape
    row = pl.BlockSpec((tm, d), lambda i: (i, 0))
    fix = lambda i: (0, 0)
    return pl.pallas_call(
        _mem_kv_kernel,
        grid=(n // tm,),
        in_specs=[row, pl.BlockSpec((1, d), fix), pl.BlockSpec((d, d), fix), pl.BlockSpec((d, d), fix)],
        out_specs=[row] * 4,
        out_shape=[jax.ShapeDtypeStruct((n, d), F32)] * 2 + [jax.ShapeDtypeStruct((n, d), BF16)] * 2,
        compiler_params=_params(("parallel",)),
        name="mem_kv",
    )(mem, g, wk_bf, wv_bf)


def _mix_residual(x, ret, dif, w_out_ref):
    half = w_out_ref.shape[0] // 2
    return x + _dot(ret.astype(BF16), w_out_ref[0:half, :]) + _dot(dif.astype(BF16), w_out_ref[half:, :])


def _cross_heads(q, mem_k, mem_v, o_ref):
    dh = q.shape[1] // MEM_HEADS
    for h in range(MEM_HEADS):
        hs = slice(h * dh, (h + 1) * dh)
        s = _dot_nt(q[:, hs], mem_k(h)) * (dh ** -0.5)
        e = jnp.exp(s - jnp.max(s, axis=-1, keepdims=True))
        p = e / jnp.sum(e, axis=-1, keepdims=True)
        o_ref[:, hs] = _dot(p.astype(BF16), mem_v(h)).astype(o_ref.dtype)


def _post_mix_kernel(x_ref, ret_ref, dif_ref, mk_ref, mv_ref, wo_ref, gc_ref, wq_ref, wm_ref, gf_ref,
                     h_ref, xt_ref, o_scr):
    h1 = _mix_residual(x_ref[...], ret_ref[...], dif_ref[...], wo_ref)
    q = _dot(_rms(h1, gc_ref[...]).astype(BF16), wq_ref[...]).astype(BF16)
    dh = q.shape[1] // MEM_HEADS
    _cross_heads(q, lambda h: mk_ref[0, :, h * dh:(h + 1) * dh], lambda h: mv_ref[0, :, h * dh:(h + 1) * dh],
                 o_scr)
    h2 = h1 + _dot(o_scr[...], wm_ref[...])
    h_ref[...] = h2
    xt_ref[...] = _rms(h2, gf_ref[...]).T.astype(BF16)


def _post_mix(x, ret, dif, mk_bf, mv_bf, wo_bf, g_cross, wq_bf, wm_bf, g_ffn, tm):
    n, d = x.shape
    nb, m, _ = mk_bf.shape
    per_batch = n // nb // tm
    row = lambda i: (i, 0)
    fix = lambda i: (0, 0)
    mem = pl.BlockSpec((1, m, d), lambda i: (i // per_batch, 0, 0))
    half = pl.BlockSpec((tm, GROUP_W), row)
    return pl.pallas_call(
        _post_mix_kernel,
        grid=(n // tm,),
        in_specs=[pl.BlockSpec((tm, d), row), half, half, mem, mem,
                  pl.BlockSpec((d, d), fix), pl.BlockSpec((1, d), fix), pl.BlockSpec((d, d), fix),
                  pl.BlockSpec((d, d), fix), pl.BlockSpec((1, d), fix)],
        out_specs=[pl.BlockSpec((tm, d), row), pl.BlockSpec((d, tm), lambda i: (0, i))],
        out_shape=[jax.ShapeDtypeStruct((n, d), F32), jax.ShapeDtypeStruct((d, n), BF16)],
        scratch_shapes=[pltpu.VMEM((tm, d), BF16)],
        compiler_params=_params(("parallel",)),
        name="post_mix",
    )(x, ret, dif, mk_bf, mv_bf, wo_bf, g_cross, wq_bf, wm_bf, g_ffn)


def _sample_mix_kernel(x_ref, ret_ref, dif_ref, wo_ref, gc_ref, wq_ref, h_ref, q_ref):
    h1 = _mix_residual(x_ref[...], ret_ref[...], dif_ref[...], wo_ref)
    h_ref[...] = h1
    q_ref[...] = _dot(_rms(h1, gc_ref[...]).astype(BF16), wq_ref[...])


def _sample_cross_kernel(q_ref, mk_ref, mv_ref, o_ref, o_scr):
    q8 = jnp.broadcast_to(q_ref[0], (8, q_ref.shape[2])).astype(BF16)
    dh = q8.shape[1] // MEM_HEADS
    _cross_heads(q8, lambda h: mk_ref[0, :, h * dh:(h + 1) * dh].astype(BF16),
                 lambda h: mv_ref[0, :, h * dh:(h + 1) * dh].astype(BF16), o_scr)
    o_ref[0] = o_scr[0:1, :]


def _sample_out_kernel(h_ref, o_ref, wm_ref, gf_ref, h2_ref, xt_ref, *, lanes):
    h2 = h_ref[...] + _dot(o_ref[...].astype(BF16), wm_ref[...])
    h2_ref[...] = h2
    xb = _rms(h2, gf_ref[...])
    pad = jnp.zeros((lanes - xb.shape[0], xb.shape[1]), F32)
    xt_ref[...] = jnp.concatenate([xb, pad], axis=0).T.astype(BF16)


def _post_mix_sample(x, ret, dif, mem_k, mem_v, wo_bf, g_cross, wq_bf, wm_bf, g_ffn, lanes):
    n, d = x.shape
    whole = lambda shape: pl.BlockSpec(shape, lambda: tuple(0 for _ in shape))
    h1, q = pl.pallas_call(
        _sample_mix_kernel,
        in_specs=[whole((n, d)), whole((n, GROUP_W)), whole((n, GROUP_W)), whole((d, d)),
                  whole((1, d)), whole((d, d))],
        out_specs=[whole((n, d)), whole((n, d))],
        out_shape=[jax.ShapeDtypeStruct((n, d), F32), jax.ShapeDtypeStruct((n, d), F32)],
        compiler_params=_params(()),
        name="sample_mix",
    )(x, ret, dif, wo_bf, g_cross, wq_bf)
    tok = pl.BlockSpec((1, 1, d), lambda b: (b, 0, 0))
    mem = pl.BlockSpec((1,) + mem_k.shape[1:], lambda b: (b, 0, 0))
    o = pl.pallas_call(
        _sample_cross_kernel,
        grid=(n,),
        in_specs=[tok, mem, mem],
        out_specs=tok,
        out_shape=jax.ShapeDtypeStruct((n, 1, d), F32),
        scratch_shapes=[pltpu.VMEM((8, d), F32)],
        compiler_params=_params(("parallel",)),
        name="sample_cross",
    )(q.reshape(n, 1, d), mem_k, mem_v)
    return pl.pallas_call(
        functools.partial(_sample_out_kernel, lanes=lanes),
        in_specs=[whole((n, d)), whole((n, d)), whole((d, d)), whole((1, d))],
        out_specs=[whole((n, d)), whole((d, lanes))],
        out_shape=[jax.ShapeDtypeStruct((n, d), F32), jax.ShapeDtypeStruct((d, lanes), BF16)],
        compiler_params=_params(()),
        name="sample_out",
    )(h1, o.reshape(n, d), wm_bf, g_ffn)


def _candidate_tiles():
    tiles = [("j", 0, 0, ()), ("j", 0, 8, ()), ("j", 1, 0, ())]
    tiles += [("i", 0, 0, (0, 1)), ("i", 0, 8, ())]
    tiles += [("i", j, 0, (0, 1)) for j in (1, 2, 3, 4)]
    return tiles


def _top_k_rows(s, k, tie_break):
    n, t = s.shape
    rows = lax.broadcasted_iota(jnp.int32, s.shape, 0).astype(F32)
    sub = lax.broadcasted_iota(jnp.int32, (8, t), 0)
    rank = jnp.full(s.shape, float(k), F32)
    blocks = [jnp.zeros((8, t), F32) for _ in range(k // 8)]
    for it in range(k):
        m = jnp.max(s, axis=0, keepdims=True)
        if tie_break:
            first = jnp.min(jnp.where(s == m, rows, float(n)), axis=0, keepdims=True)
            sel = rows == first
        else:
            sel = s == m
        rank = jnp.where(sel, float(it), rank)
        s = jnp.where(sel, -jnp.inf, s)
        blocks[it // 8] = jnp.where(sub == it % 8, m, blocks[it // 8])
    return blocks, rank


def _route_lanes(s1, s2, tie_break):
    k = PEER_TOPK
    t = s1.shape[1]
    tiles = _candidate_tiles()
    sub = lax.broadcasted_iota(jnp.int32, (8, t), 0)
    a, rank1 = _top_k_rows(s1, k, tie_break)
    b, rank2 = _top_k_rows(s2, k, tie_break)

    def row_of(blocks, i):
        return blocks[i // 8][i % 8:i % 8 + 1, :]

    cands, poss = [], []
    for kind, fixed, start, skip in tiles:
        if kind == "j":
            c = row_of(a, fixed) + b[start // 8]
            pos = fixed * k + start + sub
        else:
            c = a[start // 8] + row_of(b, fixed)
            pos = (start + sub) * k + fixed
            for r in skip:
                c = jnp.where(sub == r, -jnp.inf, c)
                pos = jnp.where(sub == r, k * k, pos)
        cands.append(c)
        poss.append(pos.astype(F32))
    c_all = jnp.concatenate(cands, axis=0)
    pos_all = jnp.concatenate(poss, axis=0)
    top = c_all[0:1, :]
    picked = jnp.zeros(c_all.shape, F32)
    work = c_all
    for _ in range(k):
        m = jnp.max(work, axis=0, keepdims=True)
        if tie_break:
            first = jnp.min(jnp.where(work == m, pos_all, float(k * k)), axis=0, keepdims=True)
            sel = pos_all == first
        else:
            sel = work == m
        picked = jnp.where(sel, 1.0, picked)
        work = jnp.where(sel, -jnp.inf, work)
    z = jnp.sum(picked * jnp.exp(jnp.where(picked > 0, c_all, top) - top), axis=0, keepdims=True)

    cnt_lo = jnp.zeros((8, t), F32)
    cnt_hi = jnp.zeros((8, t), F32)
    for n_tile, (kind, fixed, start, skip) in enumerate(tiles):
        pk = picked[n_tile * 8:(n_tile + 1) * 8, :]
        if kind == "j":
            tot = jnp.sum(pk, axis=0, keepdims=True)
            add = jnp.where(sub == fixed % 8, tot, 0.0)
            if fixed < 8:
                cnt_lo = cnt_lo + add
            else:
                cnt_hi = cnt_hi + add
        elif start == 0:
            cnt_lo = cnt_lo + pk
        else:
            cnt_hi = cnt_hi + pk
    cnt = [cnt_lo, cnt_hi]

    cnt1 = jnp.zeros(s1.shape, F32)
    for i in range(k):
        cnt1 = jnp.where(rank1 == float(i), row_of(cnt, i), cnt1)
    misses = jnp.zeros((1, t), F32)
    if not tie_break:
        taken = (jnp.sum(jnp.where(rank1 < float(k), 1.0, 0.0), axis=0, keepdims=True),
                 jnp.sum(jnp.where(rank2 < float(k), 1.0, 0.0), axis=0, keepdims=True),
                 jnp.sum(picked, axis=0, keepdims=True))
        for n_taken in taken:
            misses = misses + jnp.where(n_taken == float(k), 0.0, 1.0)
    return (rank2, jnp.exp(s2 - row_of(b, 0)), cnt1, jnp.exp(s1 - row_of(a, 0)) / z), jnp.sum(misses)


def _route_kernel(xt_ref, wq_ref, keys_ref, rank2_ref, e2w_ref, cnt1_ref, e1w_ref, q_scr):
    q_scr[...] = _dot(wq_ref[...], xt_ref[...]).astype(BF16)
    t = xt_ref.shape[1]
    lane_w = rank2_ref.shape[-1]

    def head(h, carry):
        base = pl.multiple_of(h * PEER_DQ, PEER_DQ)
        s1 = _dot(keys_ref[h, 0], q_scr[pl.ds(base, PEER_DQ // 2), :])
        s2 = _dot(keys_ref[h, 1], q_scr[pl.ds(base + PEER_DQ // 2, PEER_DQ // 2), :])

        def emit(g, tables):
            rank2, e2w, cnt1, e1w = tables
            rank2_ref[h, g] = rank2.astype(BF16)
            e2w_ref[h, g] = e2w.astype(BF16)
            cnt1_ref[h, g] = cnt1
            e1w_ref[h, g] = e1w

        groups = [slice(g * lane_w, (g + 1) * lane_w) for g in range(t // lane_w)]
        misses = []
        for g, ls in enumerate(groups):
            tables, n_miss = _route_lanes(s1[:, ls], s2[:, ls], tie_break=False)
            emit(g, tables)
            misses.append(n_miss)

        for g, ls in enumerate(groups):
            @pl.when(misses[g] > 0.0)
            def _():
                emit(g, _route_lanes(s1[:, ls], s2[:, ls], tie_break=True)[0])
        return carry

    lax.fori_loop(0, PEER_HEADS, head, 0)


def _route(xt, wq_t, keys_bf):
    d, n = xt.shape
    t = min(ROUTE_TILE, n)
    lane_w = 128
    table = pl.BlockSpec((PEER_HEADS, t // lane_w, N_KEYS, lane_w), lambda i: (0, i, 0, 0))
    shape = jax.ShapeDtypeStruct((PEER_HEADS, n // lane_w, N_KEYS, lane_w), F32)
    half_shape = jax.ShapeDtypeStruct(shape.shape, BF16)
    return pl.pallas_call(
        _route_kernel,
        grid=(n // t,),
        in_specs=[pl.BlockSpec((d, t), lambda i: (0, i)),
                  pl.BlockSpec(wq_t.shape, lambda i: (0, 0)),
                  pl.BlockSpec(keys_bf.shape, lambda i: (0, 0, 0, 0))],
        out_specs=[table] * 4,
        out_shape=[half_shape] * 2 + [shape] * 2,
        scratch_shapes=[pltpu.VMEM((PEER_HEADS * PEER_DQ, t), BF16)],
        compiler_params=_params(("parallel",)),
        name="peer_route",
    )(xt, wq_t, keys_bf)


def _peer_kernel(xt_ref, u_ref, vt_ref, rank2_in_ref, e2w_in_ref, cnt1_ref, e1w_ref, h_ref, g_ref,
                 y_ref, acc_ref, a_ref, hid_ref, rank2_ref, e2w_ref):
    s = pl.program_id(1)
    te, tm = a_ref.shape
    lane_w = rank2_ref.shape[-1]
    keys_per_tile = te // N_KEYS
    assert keys_per_tile % 8 == 0

    @pl.when(s == 0)
    def _():
        acc_ref[...] = jnp.zeros(acc_ref.shape, F32)
        rank2_ref[...] = rank2_in_ref[...]
        e2w_ref[...] = e2w_in_ref[...]

    a_ref[...] = _dot(u_ref[...], xt_ref[...])

    zero = jnp.zeros((N_KEYS, lane_w), BF16)
    for r in range(keys_per_tile):
        rs = slice(r * N_KEYS, (r + 1) * N_KEYS)
        e1_blk = pl.multiple_of(s * keys_per_tile + 8 * (r // 8), 8)
        for c in range(tm // lane_w):
            ls = slice(c * lane_w, (c + 1) * lane_w)
            gate = zero
            for h in range(PEER_HEADS):
                cnt = cnt1_ref[h, c, pl.ds(e1_blk, 8), :][r % 8:r % 8 + 1, :]
                w1 = e1w_ref[h, c, pl.ds(e1_blk, 8), :][r % 8:r % 8 + 1, :]
                cnt = jnp.broadcast_to(cnt, (N_KEYS, lane_w)).astype(BF16)
                w1 = jnp.broadcast_to(w1, (N_KEYS, lane_w)).astype(BF16)
                gate = gate + jnp.where(rank2_ref[h, c] < cnt, e2w_ref[h, c], zero) * w1
            a = a_ref[rs, ls]
            act = 0.5 * a * (1.0 + lax.erf(a * (2.0 ** -0.5)))
            hid_ref[rs, ls] = act.astype(BF16) * gate

    acc_ref[...] += _dot(vt_ref[0], hid_ref[...])

    @pl.when(s == pl.num_programs(1) - 1)
    def _():
        y_ref[...] = _rms(h_ref[...] + acc_ref[...].T, g_ref[...])


def _peer(xt, u_bf, vt_tiles, tables, h2, g_final, tm):
    d, n = xt.shape
    n_tiles, _, te = vt_tiles.shape
    lane_w = 128
    table_block = (PEER_HEADS, tm // lane_w, N_KEYS, lane_w)
    table = pl.BlockSpec(table_block, lambda i, s: (0, i, 0, 0))
    return pl.pallas_call(
        _peer_kernel,
        grid=(n // tm, n_tiles),
        in_specs=[pl.BlockSpec((d, tm), lambda i, s: (0, i)),
                  pl.BlockSpec((te, d), lambda i, s: (s, 0)),
                  pl.BlockSpec((1, d, te), lambda i, s: (s, 0, 0)),
                  table, table, table, table,
                  pl.BlockSpec((tm, d), lambda i, s: (i, 0)),
                  pl.BlockSpec((1, d), lambda i, s: (0, 0))],
        out_specs=pl.BlockSpec((tm, d), lambda i, s: (i, 0)),
        out_shape=jax.ShapeDtypeStruct((n, d), F32),
        scratch_shapes=[pltpu.VMEM((d, tm), F32), pltpu.VMEM((te, tm), F32), pltpu.VMEM((te, tm), BF16),
                        pltpu.VMEM(table_block, BF16), pltpu.VMEM(table_block, BF16)],
        compiler_params=_params(("parallel", "arbitrary")),
        name="peer_dense",
    )(xt, u_bf, vt_tiles, *tables, h2, g_final)


def _t5_bias(rel, table):
    n = jnp.maximum(rel, 0)
    max_exact = N_BUCKETS // 2
    nf = jnp.maximum(n, 1).astype(F32)
    large = max_exact + (jnp.log(nf / max_exact) / math.log(MAX_DISTANCE / max_exact)
                         * (N_BUCKETS - max_exact)).astype(jnp.int32)
    large = jnp.minimum(large, N_BUCKETS - 1)
    bucket = jnp.where(n < max_exact, n, large)
    return jnp.moveaxis(table[bucket].astype(F32), -1, 0)


def _toeplitz(vec, t):
    hh = vec.shape[0]
    flat = jnp.tile(vec, (1, t))[:, :t * (2 * t - 1)]
    return flat.reshape(hh, t, 2 * t - 1)[:, :, :t]


def _prompt_bias_tiles(rel_bias, t):
    k = jnp.arange(2 * t)
    tiles = []
    for d in (0, 1):
        rel = jnp.where(k < t, d * t - k, d * t + 2 * t - k)
        vec = _t5_bias(rel, rel_bias)
        vec = jnp.where(rel[None, :] >= 0, vec, NEG_INF)
        tiles.append(_toeplitz(vec, t))
    return jnp.stack(tiles, axis=1)


def _rotary_tables(pos):
    half = RET_DK // 2
    inv = 1.0 / (ROPE_BASE ** (jnp.arange(half, dtype=F32) / half))
    ang = pos.astype(F32)[:, None] * inv[None, :]
    cos, sin = jnp.cos(ang), jnp.sin(ang)
    return jnp.concatenate([cos, cos], axis=1), jnp.concatenate([-sin, sin], axis=1)


def _retention_tables(log_gamma):
    n = jnp.arange(RET_CHUNK, dtype=F32)
    diff = n[:, None] - n[None, :]
    decay = jnp.where(diff[None] >= 0,
                      jnp.exp(jnp.maximum(diff, 0.0)[None] * log_gamma[:, None, None]), 0.0)
    ones = jnp.ones((1, 1, HEAD_W), F32)
    qdec = jnp.exp((n + 1.0)[None, :] * log_gamma[:, None])[:, :, None] * ones
    kdec = jnp.exp((RET_CHUNK - 1.0 - n)[None, :] * log_gamma[:, None])[:, :, None] * ones
    return decay, qdec, kdec, jnp.exp(RET_CHUNK * log_gamma)


def kernel(x_prompt, x_sample, mem_prompt, cache_diff_k, cache_diff_v, page_table, state_ret,
           cache_mem_k, cache_mem_v, g_mix, w_in, w_out, lam_q1, lam_k1, lam_q2, lam_k2, rel_bias,
           g_cross, g_mem, w_mq, w_mk, w_mv, w_mo, g_ffn, w_pq, sub_keys, peer_u, peer_v, g_final):
    depth = g_mix.shape[0]
    assert depth == 1
    l = 0
    b, s, d = x_prompt.shape
    db, t_s, _ = x_sample.shape
    assert t_s == 1
    n_pages = page_table.shape[1]
    past = n_pages * PAGE
    mem_len = mem_prompt.shape[1]
    n_tok = b * s

    log_gamma = jnp.log(1.0 - 2.0 ** (-5.0 - jnp.arange(RET_HEADS, dtype=F32)))
    lam_init = 0.8 - 0.6 * math.exp(-0.3 * l)
    lam = (jnp.exp(jnp.sum(lam_q1[l].astype(F32) * lam_k1[l].astype(F32)))
           - jnp.exp(jnp.sum(lam_q2[l].astype(F32) * lam_k2[l].astype(F32))) + lam_init)
    out_scale = 1.0 - lam_init

    row = lambda v: v.reshape(1, -1)
    w_in_bf = w_in[l].astype(BF16)
    w_out_bf = w_out[l].astype(BF16)
    w_mq_bf = w_mq[l].astype(BF16)
    w_mk_bf = w_mk[l].astype(BF16)
    w_mv_bf = w_mv[l].astype(BF16)
    w_mo_bf = w_mo[l].astype(BF16)
    w_pq_t = w_pq[l].T.astype(BF16)
    keys_bf = sub_keys[l].astype(BF16)
    u_bf = peer_u[l].astype(BF16)
    n_experts = peer_v.shape[1]
    vt_bf = jnp.transpose(peer_v[l].astype(BF16).reshape(n_experts // PEER_EXPERT_TILE, PEER_EXPERT_TILE, d),
                          (0, 2, 1))
    g_fin = row(g_final)

    bias_far = _t5_bias(jnp.full((1,), MAX_DISTANCE, jnp.int32), rel_bias)[:, 0]
    bias_zero = _t5_bias(jnp.zeros((1,), jnp.int32), rel_bias)[:, 0]
    scalars = jnp.concatenate([bias_far, lam.reshape(1), bias_zero]).astype(F32)

    cs_p, sn_p = _rotary_tables(jnp.arange(s))
    rq, rk, rv, rg, dqb, dk, dv, dkb, dvb = _in_proj(
        x_prompt.reshape(n_tok, d), row(g_mix[l]), w_in_bf, cs_p, sn_p, TOKEN_TILE)
    decay, qdec, kdec, gl = _retention_tables(log_gamma)
    shp = (b, s, GROUP_W)
    ret_p, state_p = _retention_prompt(rq.reshape(shp), rk.reshape(shp), rv.reshape(shp),
                                       rg.reshape(shp), decay, qdec, kdec, gl)
    dif_p = _diff_prompt(dqb.reshape(shp), dkb.reshape(shp), dvb.reshape(shp),
                         _prompt_bias_tiles(rel_bias, ATTN_TILE), scalars, out_scale)
    mk, mv, mk_bf, mv_bf = _mem_kv(mem_prompt.reshape(b * mem_len, d), row(g_mem[l]),
                                   w_mk_bf, w_mv_bf, mem_len)
    h2_p, xt_p = _post_mix(x_prompt.reshape(n_tok, d), ret_p.reshape(n_tok, GROUP_W),
                           dif_p.reshape(n_tok, GROUP_W), mk_bf.reshape(b, mem_len, d),
                           mv_bf.reshape(b, mem_len, d), w_out_bf, row(g_cross[l]), w_mq_bf,
                           w_mo_bf, row(g_ffn[l]), TOKEN_TILE)
    tables_p = _route(xt_p, w_pq_t, keys_bf)
    y_p = _peer(xt_p, u_bf, vt_bf, tables_p, h2_p, g_fin, PEER_TOKEN_TILE)

    lanes = SAMPLE_LANES
    assert db <= lanes
    cs_s, sn_s = _rotary_tables(jnp.full((db,), past, jnp.int32))
    xs = x_sample.reshape(db, d)
    rq_s, rk_s, rv_s, rg_s, dqb_s, dk_s, dv_s, _, _ = _in_proj(
        xs, row(g_mix[l]), w_in_bf, cs_s, sn_s, db)
    tok = (db, 1, GROUP_W)
    ret_s, state_s = _retention_sample(rq_s.reshape(tok), rk_s.reshape(tok), rv_s.reshape(tok),
                                       rg_s.reshape(tok), state_ret[l], jnp.exp(log_gamma))
    near = _t5_bias(jnp.arange(MAX_DISTANCE, 0, -1), rel_bias)
    far = jnp.broadcast_to(bias_far[:, None], (DIFF_HEADS, past - MAX_DISTANCE))
    bias_past = jnp.repeat(jnp.concatenate([far, near], axis=1), 2, axis=0)
    dif_s = _diff_sample(dqb_s.astype(F32).reshape(tok), dk_s.reshape(tok), dv_s.reshape(tok),
                         jnp.transpose(cache_diff_k[l], (0, 2, 3, 4, 1)).reshape(-1, GROUP_W, PAGE),
                         cache_diff_v[l].reshape(-1, PAGE * DIFF_HEADS, HEAD_W),
                         page_table, bias_past, scalars, out_scale)
    h2_s, xt_s = _post_mix_sample(xs, ret_s.reshape(db, GROUP_W), dif_s.reshape(db, GROUP_W),
                                  cache_mem_k[l].reshape(db, mem_len, d),
                                  cache_mem_v[l].reshape(db, mem_len, d),
                                  w_out_bf, row(g_cross[l]), w_mq_bf, w_mo_bf, row(g_ffn[l]), lanes)
    tables_s = _route(xt_s, w_pq_t, keys_bf)
    h2_pad = jnp.concatenate([h2_s, jnp.zeros((lanes - db, d), F32)], axis=0)
    y_s = _peer(xt_s, u_bf, vt_bf, tables_s, h2_pad, g_fin, lanes)[:db]

    return (y_p.reshape(b, s, d), y_s.reshape(db, 1, d),
            dk.reshape(1, b, s, DIFF_HEADS, 2, DIFF_DH), dv.reshape(1, b, s, DIFF_HEADS, HEAD_W),
            state_p[None], mk.reshape(1, b, mem_len, MEM_HEADS, d // MEM_HEADS),
            mv.reshape(1, b, mem_len, MEM_HEADS, d // MEM_HEADS),
            dk_s.reshape(1, db, 1, DIFF_HEADS, 2, DIFF_DH), dv_s.reshape(1, db, 1, DIFF_HEADS, HEAD_W),
            state_s[None])
```

```python
import functools
import math

import jax
import jax.numpy as jnp
import numpy as np
from jax import lax
from jax.experimental import pallas as pl
from jax.experimental.pallas import tpu as pltpu

F32 = jnp.float32
BF16 = jnp.bfloat16

EPS = 1e-6
NEG_INF = -1e30
ROPE_BASE = 10000.0
RET_HEADS = 4
RET_DK = 128
RET_CHUNK = 128
DIFF_HEADS = 4
DIFF_DH = 64
HEAD_W = 128
GROUP_W = 512
N_GROUPS = 7
PAGE = 128
N_BUCKETS = 32
MAX_DISTANCE = 128
MEM_HEADS = 4
PEER_HEADS = 8
N_KEYS = 128
PEER_TOPK = 16
PEER_DQ = 256

VMEM_LIMIT = 56 * 1024 * 1024
ATTN_TILE = 512
ATTN_ROW_CHUNKS = 4
TOKEN_TILE = 512
PEER_EXPERT_TILE = 1024
PEER_TOKEN_TILE = 512
PEER_VMEM_LIMIT = 60 * 1024 * 1024
ROUTE_TILE = 256
SAMPLE_LANES = 128
PAGES_PER_STEP = 8


def _params(semantics, vmem=VMEM_LIMIT):
    return pltpu.CompilerParams(dimension_semantics=semantics, vmem_limit_bytes=vmem)


def _rms(x, g):
    return (x * lax.rsqrt(jnp.mean(x * x, axis=-1, keepdims=True) + EPS)) * g


def _dot(a, b):
    return jnp.dot(a, b, preferred_element_type=F32)


def _dot_nt(a, b):
    return lax.dot_general(a, b, (((1,), (1,)), ((), ())), preferred_element_type=F32)


def _in_proj_kernel(x_ref, g_ref, w_ref, cs_ref, sn_ref,
                    rq_ref, rk_ref, rv_ref, rg_ref, dqb_ref, dk_ref, dv_ref, dkb_ref, dvb_ref):
    xb = _rms(x_ref[...], g_ref[...]).astype(BF16)
    cs = cs_ref[...]
    sn = sn_ref[...]

    def proj(c):
        return _dot(xb, w_ref[:, c * GROUP_W:(c + 1) * GROUP_W])

    def rotary(p, out_ref, scale):
        for h in range(RET_HEADS):
            ph = p[:, h * HEAD_W:(h + 1) * HEAD_W]
            r = ph * cs + pltpu.roll(ph, HEAD_W // 2, 1) * sn
            if scale is not None:
                r = r * scale
            out_ref[:, h * HEAD_W:(h + 1) * HEAD_W] = r

    rotary(proj(0), rq_ref, RET_DK ** -0.5)
    rotary(proj(1), rk_ref, None)
    rv_ref[...] = proj(2)
    rg_ref[...] = proj(3)
    dqb_ref[...] = (proj(4) * (DIFF_DH ** -0.5)).astype(BF16)
    dk = proj(5)
    dk_ref[...] = dk
    dkb_ref[...] = dk.astype(BF16)
    dv = proj(6)
    dv_ref[...] = dv
    dvb_ref[...] = dv.astype(BF16)


def _in_proj(x, g, w_bf, cs, sn, tm):
    n, d = x.shape
    nt = n // tm
    n_pos = cs.shape[0] // tm
    row = lambda i: (i, 0)
    pos = lambda i: (i % n_pos, 0)
    fix = lambda i: (0, 0)
    wide = pl.BlockSpec((tm, GROUP_W), row)
    f32o = jax.ShapeDtypeStruct((n, GROUP_W), F32)
    bf16o = jax.ShapeDtypeStruct((n, GROUP_W), BF16)
    return pl.pallas_call(
        _in_proj_kernel,
        grid=(nt,),
        in_specs=[pl.BlockSpec((tm, d), row), pl.BlockSpec((1, d), fix),
                  pl.BlockSpec((d, N_GROUPS * GROUP_W), fix),
                  pl.BlockSpec((tm, HEAD_W), pos), pl.BlockSpec((tm, HEAD_W), pos)],
        out_specs=[wide] * 9,
        out_shape=[f32o, f32o, f32o, f32o, bf16o, f32o, f32o, bf16o, bf16o],
        compiler_params=_params(("parallel",)),
        name="in_proj",
    )(x, g, w_bf, cs, sn)


def _retention_kernel(gl_ref, rq_ref, rk_ref, rv_ref, rg_ref, dec_ref, qd_ref, kd_ref,
                      ret_ref, st_ref):
    c = pl.program_id(0)

    @pl.when(c == 0)
    def _():
        st_ref[...] = jnp.zeros(st_ref.shape, F32)

    nb = rq_ref.shape[0]
    for b in range(nb):
        for h in range(RET_HEADS):
            hs = slice(h * HEAD_W, (h + 1) * HEAD_W)
            q = rq_ref[b, :, hs].astype(BF16)
            k = rk_ref[b, :, hs]
            v = rv_ref[b, :, hs].astype(BF16)
            state = st_ref[b, h]
            scores = _dot_nt(q, k.astype(BF16)) * dec_ref[h]
            inner = _dot(scores.astype(BF16), v)
            cross = _dot(q, state.astype(BF16)) * qd_ref[h]
            o = inner + cross
            kd = (k * kd_ref[h]).T.astype(BF16)
            st_ref[b, h] = gl_ref[h] * state + _dot(kd, v)
            rg = rg_ref[b, :, hs]
            o = o * lax.rsqrt(jnp.mean(o * o, axis=-1, keepdims=True) + EPS)
            ret_ref[b, :, hs] = (o * (rg * jax.nn.sigmoid(rg))).astype(BF16)


def _retention_prompt(rq, rk, rv, rg, decay, qdec, kdec, gl):
    b, s, w = rq.shape
    nc = s // RET_CHUNK
    tok = pl.BlockSpec((b, RET_CHUNK, w), lambda c: (0, c, 0))
    const = pl.BlockSpec((RET_HEADS, RET_CHUNK, RET_CHUNK), lambda c: (0, 0, 0))
    return pl.pallas_call(
        _retention_kernel,
        grid=(nc,),
        in_specs=[pl.BlockSpec(memory_space=pltpu.SMEM), tok, tok, tok, tok, const, const, const],
        out_specs=[tok, pl.BlockSpec((b, RET_HEADS, RET_DK, HEAD_W), lambda c: (0, 0, 0, 0))],
        out_shape=[jax.ShapeDtypeStruct((b, s, w), BF16),
                   jax.ShapeDtypeStruct((b, RET_HEADS, RET_DK, HEAD_W), F32)],
        compiler_params=_params(("arbitrary",)),
        name="retention_prompt",
    )(gl, rq, rk, rv, rg, decay, qdec, kdec)


def _retention_step_kernel(gam_ref, rq_ref, rk_ref, rv_ref, rg_ref, st_ref, ret_ref, nst_ref):
    rows = lax.broadcasted_iota(jnp.int32, (RET_DK, HEAD_W), 0)
    for h in range(RET_HEADS):
        hs = slice(h * HEAD_W, (h + 1) * HEAD_W)
        q = rq_ref[0, :, hs]
        k = rk_ref[0, :, hs]
        v = rv_ref[0, :, hs]
        rg = rg_ref[0, :, hs]
        state = st_ref[0, h]
        gam = gam_ref[h]
        q8 = jnp.broadcast_to(q, (8, HEAD_W)).astype(BF16)
        qk = jnp.sum(q.astype(BF16).astype(F32) * k.astype(BF16).astype(F32), axis=-1, keepdims=True)
        cross = _dot(q8, state.astype(BF16))[0:1, :] * gam
        o = qk * v.astype(BF16).astype(F32) + cross
        k_first = jnp.where(rows == 0, jnp.broadcast_to(k, (RET_DK, HEAD_W)), 0.0)
        v_first = jnp.where(rows == 0, jnp.broadcast_to(v, (RET_DK, HEAD_W)), 0.0)
        nst_ref[0, h] = gam * state + _dot(k_first.T.astype(BF16), v_first.astype(BF16))
        o = o * lax.rsqrt(jnp.mean(o * o, axis=-1, keepdims=True) + EPS)
        ret_ref[0, :, hs] = o * (rg * jax.nn.sigmoid(rg))


def _retention_sample(rq, rk, rv, rg, state, gam):
    nb = rq.shape[0]
    tok = pl.BlockSpec((1, 1, GROUP_W), lambda b: (b, 0, 0))
    st = pl.BlockSpec((1, RET_HEADS, RET_DK, HEAD_W), lambda b: (b, 0, 0, 0))
    return pl.pallas_call(
        _retention_step_kernel,
        grid=(nb,),
        in_specs=[pl.BlockSpec(memory_space=pltpu.SMEM), tok, tok, tok, tok, st],
        out_specs=[tok, st],
        out_shape=[jax.ShapeDtypeStruct((nb, 1, GROUP_W), F32),
                   jax.ShapeDtypeStruct(state.shape, F32)],
        compiler_params=_params(("parallel",)),
        name="retention_sample",
    )(gam, rq, rk, rv, rg, state)


def _diff_prompt_kernel(qi_ref, kj_ref, sc_ref, q_ref, k_ref, v_ref, b_ref, o_ref,
                        q2_ref, m_ref, l_ref, acc_ref, *, out_scale):
    h = pl.program_id(1)
    step = pl.program_id(2)
    i = qi_ref[step]
    j = kj_ref[step]
    t = q_ref.shape[1]
    rows = 2 * t // ATTN_ROW_CHUNKS

    @pl.when(j == 0)
    def _():
        q = q_ref[0]
        lane = lax.broadcasted_iota(jnp.int32, q.shape, 1)
        zero = jnp.zeros_like(q)
        q2_ref[0:t, :] = jnp.where(lane < DIFF_DH, q, zero)
        q2_ref[t:2 * t, :] = jnp.where(lane < DIFF_DH, zero, q)
        m_ref[...] = jnp.full(m_ref.shape, NEG_INF, F32)
        l_ref[...] = jnp.zeros(l_ref.shape, F32)
        acc_ref[...] = jnp.zeros(acc_ref.shape, F32)

    def update(bias_of):
        k = k_ref[0]
        v = v_ref[0]
        for c in range(ATTN_ROW_CHUNKS):
            rs = slice(c * rows, (c + 1) * rows)
            s = _dot_nt(q2_ref[rs, :], k) + bias_of((c * rows) % t, rows)
            m_prev = m_ref[rs, :]
            m_new = jnp.maximum(m_prev, jnp.max(s, axis=-1, keepdims=True))
            alpha = jnp.exp(m_prev - m_new)
            p = jnp.exp(s - m_new)
            l_ref[rs, :] = alpha * l_ref[rs, :] + jnp.sum(p, axis=-1, keepdims=True)
            acc_ref[rs, :] = alpha * acc_ref[rs, :] + _dot(p.astype(BF16), v)
            m_ref[rs, :] = m_new

    d = i - j

    @pl.when(d >= 2)
    def _():
        far = sc_ref[h]
        update(lambda r0, n: far)

    @pl.when(d == 1)
    def _():
        update(lambda r0, n: b_ref[0, 1, r0:r0 + n, :])

    @pl.when(d == 0)
    def _():
        update(lambda r0, n: b_ref[0, 0, r0:r0 + n, :])
        lam = sc_ref[DIFF_HEADS]
        o = acc_ref[...] / l_ref[...]
        o = o[0:t] - lam * o[t:2 * t]
        o = o * lax.rsqrt(jnp.mean(o * o, axis=-1, keepdims=True) + EPS)
        o_ref[0] = (o * out_scale).astype(BF16)


def _diff_prompt(dqb, dkb, dvb, bias_tiles, scalars, out_scale):
    b, s, w = dqb.shape
    t = bias_tiles.shape[-1]
    n = s // t
    pairs = [(i, j) for i in range(n) for j in range(i + 1)]
    qi = jnp.asarray([p[0] for p in pairs], jnp.int32)
    kj = jnp.asarray([p[1] for p in pairs], jnp.int32)
    qspec = pl.BlockSpec((1, t, HEAD_W), lambda b_, h, st, qi_, kj_: (b_, qi_[st], h))
    kspec = pl.BlockSpec((1, t, HEAD_W), lambda b_, h, st, qi_, kj_: (b_, kj_[st], h))
    grid_spec = pltpu.PrefetchScalarGridSpec(
        num_scalar_prefetch=2,
        grid=(b, DIFF_HEADS, len(pairs)),
        in_specs=[pl.BlockSpec(memory_space=pltpu.SMEM), qspec, kspec, kspec,
                  pl.BlockSpec((1, 2, t, t), lambda b_, h, st, qi_, kj_: (h, 0, 0, 0))],
        out_specs=qspec,
        scratch_shapes=[pltpu.VMEM((2 * t, HEAD_W), BF16), pltpu.VMEM((2 * t, 1), F32),
                        pltpu.VMEM((2 * t, 1), F32), pltpu.VMEM((2 * t, HEAD_W), F32)],
    )
    return pl.pallas_call(
        functools.partial(_diff_prompt_kernel, out_scale=out_scale),
        grid_spec=grid_spec,
        out_shape=jax.ShapeDtypeStruct((b, s, w), BF16),
        compiler_params=_params(("parallel", "parallel", "arbitrary")),
        name="diff_attn_prompt",
    )(qi, kj, scalars, dqb, dkb, dvb, bias_tiles)


def _diff_sample_kernel(pt_ref, sc_ref, q_ref, kn_ref, vn_ref, bias_ref, *rest, n_pp, out_scale):
    k_refs = rest[:n_pp]
    v_refs = rest[n_pp:2 * n_pp]
    o_ref, qm_ref, m_ref, l_ref, acc_ref = rest[2 * n_pp:]
    j = pl.program_id(1)
    n_rows = 2 * DIFF_HEADS

    @pl.when(j == 0)
    def _():
        q = jnp.broadcast_to(q_ref[0], (n_rows, GROUP_W))
        lane = lax.broadcasted_iota(jnp.int32, (n_rows, GROUP_W), 1)
        row = lax.broadcasted_iota(jnp.int32, (n_rows, GROUP_W), 0)
        qm = jnp.where(lax.shift_right_logical(lane, int(math.log2(DIFF_DH))) == row, q, 0.0)
        qm_ref[...] = qm
        kn = kn_ref[0].astype(BF16).astype(F32)
        s_new = jnp.sum(qm * kn, axis=-1, keepdims=True)
        head = lax.shift_right_logical(lax.broadcasted_iota(jnp.int32, (n_rows, 1), 0), 1)
        b0 = jnp.zeros((n_rows, 1), F32)
        for hh in range(DIFF_HEADS):
            b0 = jnp.where(head == hh, sc_ref[DIFF_HEADS + 1 + hh], b0)
        m_ref[...] = s_new + b0
        l_ref[...] = jnp.ones(l_ref.shape, F32)
        acc_ref[...] = jnp.broadcast_to(vn_ref[0].astype(BF16).astype(F32), (n_rows, GROUP_W))

    qm = qm_ref[...].astype(BF16)
    s = jnp.concatenate([_dot(qm, k_refs[u][0].astype(BF16)) for u in range(n_pp)], axis=1)
    s = s + bias_ref[...]
    m_prev = m_ref[...]
    m_new = jnp.maximum(m_prev, jnp.max(s, axis=-1, keepdims=True))
    alpha = jnp.exp(m_prev - m_new)
    p = jnp.exp(s - m_new)
    l_ref[...] = alpha * l_ref[...] + jnp.sum(p, axis=-1, keepdims=True)
    pb = p.astype(BF16)
    for hh in range(DIFF_HEADS):
        hs = slice(hh * HEAD_W, (hh + 1) * HEAD_W)
        pv = None
        for u in range(n_pp):
            v_h = v_refs[u][0, pl.ds(hh, PAGE, stride=DIFF_HEADS), :].astype(BF16)
            d = _dot(pb[:, u * PAGE:(u + 1) * PAGE], v_h)
            pv = d if pv is None else pv + d
        acc_ref[:, hs] = alpha * acc_ref[:, hs] + pv
    m_ref[...] = m_new

    @pl.when(j == pl.num_programs(1) - 1)
    def _():
        lam = sc_ref[DIFF_HEADS]
        o = acc_ref[...] / l_ref[...]
        for hh in range(DIFF_HEADS):
            hs = slice(hh * HEAD_W, (hh + 1) * HEAD_W)
            oh = o[2 * hh:2 * hh + 1, hs] - lam * o[2 * hh + 1:2 * hh + 2, hs]
            oh = oh * lax.rsqrt(jnp.mean(oh * oh, axis=-1, keepdims=True) + EPS)
            o_ref[0, :, hs] = oh * out_scale


def _diff_sample(dqb, dk_new, dv_new, cache_k, cache_v, page_table, bias_past, scalars, out_scale):
    nb = dqb.shape[0]
    n_pages = page_table.shape[1]
    n_pp = PAGES_PER_STEP
    while n_pages % n_pp:
        n_pp //= 2
    tok = pl.BlockSpec((1, 1, GROUP_W), lambda b, j, pt: (b, 0, 0))

    def page_spec(u):
        return pl.BlockSpec((1, GROUP_W, PAGE), lambda b, j, pt: (pt[b, j * n_pp + u], 0, 0))

    n_rows = 2 * DIFF_HEADS
    grid_spec = pltpu.PrefetchScalarGridSpec(
        num_scalar_prefetch=1,
        grid=(nb, n_pages // n_pp),
        in_specs=[pl.BlockSpec(memory_space=pltpu.SMEM), tok, tok, tok,
                  pl.BlockSpec((n_rows, n_pp * PAGE), lambda b, j, pt: (0, j))]
                 + [page_spec(u) for u in range(n_pp)] * 2,
        out_specs=tok,
        scratch_shapes=[pltpu.VMEM((n_rows, GROUP_W), F32), pltpu.VMEM((n_rows, 1), F32),
                        pltpu.VMEM((n_rows, 1), F32), pltpu.VMEM((n_rows, GROUP_W), F32)],
    )
    return pl.pallas_call(
        functools.partial(_diff_sample_kernel, n_pp=n_pp, out_scale=out_scale),
        grid_spec=grid_spec,
        out_shape=jax.ShapeDtypeStruct((nb, 1, GROUP_W), F32),
        compiler_params=_params(("parallel", "arbitrary")),
        name="diff_attn_sample",
    )(page_table, scalars, dqb, dk_new, dv_new, bias_past,
      *([cache_k] * n_pp), *([cache_v] * n_pp))


def _mem_kv_kernel(m_ref, g_ref, wk_ref, wv_ref, k_ref, v_ref, kb_ref, vb_ref):
    mn = _rms(m_ref[...], g_ref[...]).astype(BF16)
    k = _dot(mn, wk_ref[...])
    v = _dot(mn, wv_ref[...])
    k_ref[...] = k
    v_ref[...] = v
    kb_ref[...] = k.astype(BF16)
    vb_ref[...] = v.astype(BF16)


def _mem_kv(mem, g, wk_bf, wv_bf, tm):
    n, d = mem.shape
    row = pl.BlockSpec((tm, d), lambda i: (i, 0))
    fix = lambda i: (0, 0)
    return pl.pallas_call(
        _mem_kv_kernel,
        grid=(n // tm,),
        in_specs=[row, pl.BlockSpec((1, d), fix), pl.BlockSpec((d, d), fix), pl.BlockSpec((d, d), fix)],
        out_specs=[row] * 4,
        out_shape=[jax.ShapeDtypeStruct((n, d), F32)] * 2 + [jax.ShapeDtypeStruct((n, d), BF16)] * 2,
        compiler_params=_params(("parallel",)),
        name="mem_kv",
    )(mem, g, wk_bf, wv_bf)


def _mix_residual(x, ret, dif, w_out_ref):
    half = w_out_ref.shape[0] // 2
    return x + _dot(ret.astype(BF16), w_out_ref[0:half, :]) + _dot(dif.astype(BF16), w_out_ref[half:, :])


def _cross_heads(q, mem_k, mem_v, o_ref):
    dh = q.shape[1] // MEM_HEADS
    for h in range(MEM_HEADS):
        hs = slice(h * dh, (h + 1) * dh)
        s = _dot_nt(q[:, hs], mem_k(h)) * (dh ** -0.5)
        e = jnp.exp(s - jnp.max(s, axis=-1, keepdims=True))
        p = e / jnp.sum(e, axis=-1, keepdims=True)
        o_ref[:, hs] = _dot(p.astype(BF16), mem_v(h)).astype(o_ref.dtype)


def _post_mix_kernel(x_ref, ret_ref, dif_ref, mk_ref, mv_ref, wo_ref, gc_ref, wq_ref, wm_ref, gf_ref,
                     h_ref, xt_ref, o_scr):
    h1 = _mix_residual(x_ref[...], ret_ref[...], dif_ref[...], wo_ref)
    q = _dot(_rms(h1, gc_ref[...]).astype(BF16), wq_ref[...]).astype(BF16)
    dh = q.shape[1] // MEM_HEADS
    _cross_heads(q, lambda h: mk_ref[0, :, h * dh:(h + 1) * dh], lambda h: mv_ref[0, :, h * dh:(h + 1) * dh],
                 o_scr)
    h2 = h1 + _dot(o_scr[...], wm_ref[...])
    h_ref[...] = h2
    xt_ref[...] = _rms(h2, gf_ref[...]).T.astype(BF16)


def _post_mix(x, ret, dif, mk_bf, mv_bf, wo_bf, g_cross, wq_bf, wm_bf, g_ffn, tm):
    n, d = x.shape
    nb, m, _ = mk_bf.shape
    per_batch = n // nb // tm
    row = lambda i: (i, 0)
    fix = lambda i: (0, 0)
    mem = pl.BlockSpec((1, m, d), lambda i: (i // per_batch, 0, 0))
    half = pl.BlockSpec((tm, GROUP_W), row)
    return pl.pallas_call(
        _post_mix_kernel,
        grid=(n // tm,),
        in_specs=[pl.BlockSpec((tm, d), row), half, half, mem, mem,
                  pl.BlockSpec((d, d), fix), pl.BlockSpec((1, d), fix), pl.BlockSpec((d, d), fix),
                  pl.BlockSpec((d, d), fix), pl.BlockSpec((1, d), fix)],
        out_specs=[pl.BlockSpec((tm, d), row), pl.BlockSpec((d, tm), lambda i: (0, i))],
        out_shape=[jax.ShapeDtypeStruct((n, d), F32), jax.ShapeDtypeStruct((d, n), BF16)],
        scratch_shapes=[pltpu.VMEM((tm, d), BF16)],
        compiler_params=_params(("parallel",)),
        name="post_mix",
    )(x, ret, dif, mk_bf, mv_bf, wo_bf, g_cross, wq_bf, wm_bf, g_ffn)


def _sample_mix_kernel(x_ref, ret_ref, dif_ref, wo_ref, gc_ref, wq_ref, h_ref, q_ref):
    h1 = _mix_residual(x_ref[...], ret_ref[...], dif_ref[...], wo_ref)
    h_ref[...] = h1
    q_ref[...] = _dot(_rms(h1, gc_ref[...]).astype(BF16), wq_ref[...])


def _sample_cross_kernel(q_ref, mk_ref, mv_ref, o_ref, o_scr):
    q8 = jnp.broadcast_to(q_ref[0], (8, q_ref.shape[2])).astype(BF16)
    dh = q8.shape[1] // MEM_HEADS
    _cross_heads(q8, lambda h: mk_ref[0, :, h * dh:(h + 1) * dh].astype(BF16),
                 lambda h: mv_ref[0, :, h * dh:(h + 1) * dh].astype(BF16), o_scr)
    o_ref[0] = o_scr[0:1, :]


def _sample_out_kernel(h_ref, o_ref, wm_ref, gf_ref, h2_ref, xt_ref, *, lanes):
    h2 = h_ref[...] + _dot(o_ref[...].astype(BF16), wm_ref[...])
    h2_ref[...] = h2
    xb = _rms(h2, gf_ref[...])
    pad = jnp.zeros((lanes - xb.shape[0], xb.shape[1]), F32)
    xt_ref[...] = jnp.concatenate([xb, pad], axis=0).T.astype(BF16)


def _post_mix_sample(x, ret, dif, mem_k, mem_v, wo_bf, g_cross, wq_bf, wm_bf, g_ffn, lanes):
    n, d = x.shape
    whole = lambda shape: pl.BlockSpec(shape, lambda: tuple(0 for _ in shape))
    h1, q = pl.pallas_call(
        _sample_mix_kernel,
        in_specs=[whole((n, d)), whole((n, GROUP_W)), whole((n, GROUP_W)), whole((d, d)),
                  whole((1, d)), whole((d, d))],
        out_specs=[whole((n, d)), whole((n, d))],
        out_shape=[jax.ShapeDtypeStruct((n, d), F32), jax.ShapeDtypeStruct((n, d), F32)],
        compiler_params=_params(()),
        name="sample_mix",
    )(x, ret, dif, wo_bf, g_cross, wq_bf)
    tok = pl.BlockSpec((1, 1, d), lambda b: (b, 0, 0))
    mem = pl.BlockSpec((1,) + mem_k.shape[1:], lambda b: (b, 0, 0))
    o = pl.pallas_call(
        _sample_cross_kernel,
        grid=(n,),
        in_specs=[tok, mem, mem],
        out_specs=tok,
        out_shape=jax.ShapeDtypeStruct((n, 1, d), F32),
        scratch_shapes=[pltpu.VMEM((8, d), F32)],
        compiler_params=_params(("parallel",)),
        name="sample_cross",
    )(q.reshape(n, 1, d), mem_k, mem_v)
    return pl.pallas_call(
        functools.partial(_sample_out_kernel, lanes=lanes),
        in_specs=[whole((n, d)), whole((n, d)), whole((d, d)), whole((1, d))],
        out_specs=[whole((n, d)), whole((d, lanes))],
        out_shape=[jax.ShapeDtypeStruct((n, d), F32), jax.ShapeDtypeStruct((d, lanes), BF16)],
        compiler_params=_params(()),
        name="sample_out",
    )(h1, o.reshape(n, d), wm_bf, g_ffn)


def _candidate_tiles():
    tiles = [("j", 0, 0, ()), ("j", 0, 8, ()), ("j", 1, 0, ())]
    tiles += [("i", 0, 0, (0, 1)), ("i", 0, 8, ())]
    tiles += [("i", j, 0, (0, 1)) for j in (1, 2, 3, 4)]
    return tiles


def _top_k_rows(s, k, tie_break):
    n, t = s.shape
    rows = lax.broadcasted_iota(jnp.int32, s.shape, 0).astype(F32)
    sub = lax.broadcasted_iota(jnp.int32, (8, t), 0)
    rank = jnp.full(s.shape, float(k), F32)
    blocks = [jnp.zeros((8, t), F32) for _ in range(k // 8)]
    for it in range(k):
        m = jnp.max(s, axis=0, keepdims=True)
        if tie_break:
            first = jnp.min(jnp.where(s == m, rows, float(n)), axis=0, keepdims=True)
            sel = rows == first
        else:
            sel = s == m
        rank = jnp.where(sel, float(it), rank)
        s = jnp.where(sel, -jnp.inf, s)
        blocks[it // 8] = jnp.where(sub == it % 8, m, blocks[it // 8])
    return blocks, rank


def _route_lanes(s1, s2, tie_break):
    k = PEER_TOPK
    t = s1.shape[1]
    tiles = _candidate_tiles()
    sub = lax.broadcasted_iota(jnp.int32, (8, t), 0)
    a, rank1 = _top_k_rows(s1, k, tie_break)
    b, rank2 = _top_k_rows(s2, k, tie_break)

    def row_of(blocks, i):
        return blocks[i // 8][i % 8:i % 8 + 1, :]

    cands, poss = [], []
    for kind, fixed, start, skip in tiles:
        if kind == "j":
            c = row_of(a, fixed) + b[start // 8]
            pos = fixed * k + start + sub
        else:
            c = a[start // 8] + row_of(b, fixed)
            pos = (start + sub) * k + fixed
            for r in skip:
                c = jnp.where(sub == r, -jnp.inf, c)
                pos = jnp.where(sub == r, k * k, pos)
        cands.append(c)
        poss.append(pos.astype(F32))
    c_all = jnp.concatenate(cands, axis=0)
    pos_all = jnp.concatenate(poss, axis=0)
    top = c_all[0:1, :]
    picked = jnp.zeros(c_all.shape, F32)
    work = c_all
    for _ in range(k):
        m = jnp.max(work, axis=0, keepdims=True)
        if tie_break:
            first = jnp.min(jnp.where(work == m, pos_all, float(k * k)), axis=0, keepdims=True)
            sel = pos_all == first
        else:
            sel = work == m
        picked = jnp.where(sel, 1.0, picked)
        work = jnp.where(sel, -jnp.inf, work)
    z = jnp.sum(picked * jnp.exp(jnp.where(picked > 0, c_all, top) - top), axis=0, keepdims=True)

    cnt_lo = jnp.zeros((8, t), F32)
    cnt_hi = jnp.zeros((8, t), F32)
    for n_tile, (kind, fixed, start, skip) in enumerate(tiles):
        pk = picked[n_tile * 8:(n_tile + 1) * 8, :]
        if kind == "j":
            tot = jnp.sum(pk, axis=0, keepdims=True)
            add = jnp.where(sub == fixed % 8, tot, 0.0)
            if fixed < 8:
                cnt_lo = cnt_lo + add
            else:
                cnt_hi = cnt_hi + add
        elif start == 0:
            cnt_lo = cnt_lo + pk
        else:
            cnt_hi = cnt_hi + pk
    cnt = [cnt_lo, cnt_hi]

    cnt1 = jnp.zeros(s1.shape, F32)
    for i in range(k):
        cnt1 = jnp.where(rank1 == float(i), row_of(cnt, i), cnt1)
    misses = jnp.zeros((1, t), F32)
    if not tie_break:
        taken = (jnp.sum(jnp.where(rank1 < float(k), 1.0, 0.0), axis=0, keepdims=True),
                 jnp.sum(jnp.where(rank2 < float(k), 1.0, 0.0), axis=0, keepdims=True),
                 jnp.sum(picked, axis=0, keepdims=True))
        for n_taken in taken:
            misses = misses + jnp.where(n_taken == float(k), 0.0, 1.0)
    return (rank2, jnp.exp(s2 - row_of(b, 0)), cnt1, jnp.exp(s1 - row_of(a, 0)) / z), jnp.sum(misses)


def _route_kernel(xt_ref, wq_ref, keys_ref, rank2_ref, e2w_ref, cnt1_ref, e1w_ref, q_scr):
    q_scr[...] = _dot(wq_ref[...], xt_ref[...]).astype(BF16)
    t = xt_ref.shape[1]
    lane_w = rank2_ref.shape[-1]

    def head(h, carry):
        base = pl.multiple_of(h * PEER_DQ, PEER_DQ)
        s1 = _dot(keys_ref[h, 0], q_scr[pl.ds(base, PEER_DQ // 2), :])
        s2 = _dot(keys_ref[h, 1], q_scr[pl.ds(base + PEER_DQ // 2, PEER_DQ // 2), :])

        def emit(g, tables):
            rank2, e2w, cnt1, e1w = tables
            rank2_ref[h, g] = rank2.astype(BF16)
            e2w_ref[h, g] = e2w.astype(BF16)
            cnt1_ref[h, g] = cnt1
            e1w_ref[h, g] = e1w

        groups = [slice(g * lane_w, (g + 1) * lane_w) for g in range(t // lane_w)]
        misses = []
        for g, ls in enumerate(groups):
            tables, n_miss = _route_lanes(s1[:, ls], s2[:, ls], tie_break=False)
            emit(g, tables)
            misses.append(n_miss)

        for g, ls in enumerate(groups):
            @pl.when(misses[g] > 0.0)
            def _():
                emit(g, _route_lanes(s1[:, ls], s2[:, ls], tie_break=True)[0])
        return carry

    lax.fori_loop(0, PEER_HEADS, head, 0)


def _route(xt, wq_t, keys_bf):
    d, n = xt.shape
    t = min(ROUTE_TILE, n)
    lane_w = 128
    table = pl.BlockSpec((PEER_HEADS, t // lane_w, N_KEYS, lane_w), lambda i: (0, i, 0, 0))
    shape = jax.ShapeDtypeStruct((PEER_HEADS, n // lane_w, N_KEYS, lane_w), F32)
    half_shape = jax.ShapeDtypeStruct(shape.shape, BF16)
    return pl.pallas_call(
        _route_kernel,
        grid=(n // t,),
        in_specs=[pl.BlockSpec((d, t), lambda i: (0, i)),
                  pl.BlockSpec(wq_t.shape, lambda i: (0, 0)),
                  pl.BlockSpec(keys_bf.shape, lambda i: (0, 0, 0, 0))],
        out_specs=[table] * 4,
        out_shape=[half_shape] * 2 + [shape] * 2,
        scratch_shapes=[pltpu.VMEM((PEER_HEADS * PEER_DQ, t), BF16)],
        compiler_params=_params(("parallel",)),
        name="peer_route",
    )(xt, wq_t, keys_bf)


def _gelu(x):
    z = jnp.abs(x) * (2.0 ** -0.5)
    t = 1.0 / (1.0 + 0.3275911 * z)
    poly = t * (0.254829592 + t * (-0.284496736 + t * (1.421413741 + t * (-1.453152027 + t * 1.061405429))))
    half_erfc = 0.5 * poly * jnp.exp(-(z * z))
    return x * jnp.where(x >= 0.0, 1.0 - half_erfc, half_erfc)


def _peer_gate(a_ref, hid_ref, step, odd, rank2_ref, e2w_ref, cnt1_ref, e1w_ref):
    keys_per_tile, n_groups, _, lane_w = a_ref.shape
    assert (2 * keys_per_tile) % 8 == 0
    zero = jnp.zeros((N_KEYS, lane_w), BF16)
    for r in range(keys_per_tile):
        rs = slice(r * N_KEYS, (r + 1) * N_KEYS)
        e1 = odd * keys_per_tile + r
        e1_blk = pl.multiple_of(step * 2 * keys_per_tile + 8 * (e1 // 8), 8)
        for c in range(n_groups):
            ls = slice(c * lane_w, (c + 1) * lane_w)
            gate = zero
            for h in range(PEER_HEADS):
                cnt = cnt1_ref[h, c, pl.ds(e1_blk, 8), :][e1 % 8:e1 % 8 + 1, :]
                w1 = e1w_ref[h, c, pl.ds(e1_blk, 8), :][e1 % 8:e1 % 8 + 1, :]
                cnt = jnp.broadcast_to(cnt, (N_KEYS, lane_w)).astype(BF16)
                w1 = jnp.broadcast_to(w1, (N_KEYS, lane_w)).astype(BF16)
                gate = gate + jnp.where(rank2_ref[h, c] < cnt, e2w_ref[h, c], zero) * w1
            hid_ref[rs, ls] = _gelu(a_ref[r, c]).astype(BF16) * gate


def _store_blocks(a_ref, a):
    keys_per_tile, n_groups, keys, lane_w = a_ref.shape
    for r in range(keys_per_tile):
        for c in range(n_groups):
            a_ref[r, c] = a[r * keys:(r + 1) * keys, c * lane_w:(c + 1) * lane_w]


def _peer_kernel(xt_ref, u_first_ref, u_odd_ref, u_next_ref, vt_prev_ref, vt_even_ref, vt_last_ref,
                 rank2_in_ref, e2w_in_ref, cnt1_ref, e1w_ref, h_ref, g_ref,
                 y_ref, acc_ref, a0_ref, a1_ref, hid0_ref, hid1_ref, rank2_ref, e2w_ref):
    s = pl.program_id(1)
    tables = (rank2_ref, e2w_ref, cnt1_ref, e1w_ref)
    xt = xt_ref[...]

    @pl.when(s == 0)
    def _():
        acc_ref[...] = jnp.zeros(acc_ref.shape, F32)
        hid1_ref[...] = jnp.zeros(hid1_ref.shape, BF16)
        _store_blocks(a0_ref, _dot(u_first_ref[...], xt))
        rank2_ref[...] = rank2_in_ref[...]
        e2w_ref[...] = e2w_in_ref[...]

    _peer_gate(a0_ref, hid0_ref, s, 0, *tables)
    out = _dot(vt_prev_ref[0], hid1_ref[...])
    _store_blocks(a1_ref, _dot(u_odd_ref[...], xt))
    _peer_gate(a1_ref, hid1_ref, s, 1, *tables)
    out = out + _dot(vt_even_ref[0], hid0_ref[...])
    _store_blocks(a0_ref, _dot(u_next_ref[...], xt))
    acc_ref[...] += out

    @pl.when(s == pl.num_programs(1) - 1)
    def _():
        acc = acc_ref[...] + _dot(vt_last_ref[0], hid1_ref[...])
        y_ref[...] = _rms(h_ref[...] + acc.T, g_ref[...])


def _peer(xt, u_bf, vt_tiles, tables, h2, g_final, tm):
    d, n = xt.shape
    n_tiles, _, te = vt_tiles.shape
    n_steps = n_tiles // 2
    assert n_tiles == 2 * n_steps
    lane_w = 128
    n_groups = tm // lane_w
    once = pl.Buffered(1)
    table_block = (PEER_HEADS, n_groups, N_KEYS, lane_w)
    table = pl.BlockSpec(table_block, lambda i, s: (0, i, 0, 0))
    u_spec = lambda tile, mode=None: pl.BlockSpec((te, d), lambda i, s: (tile(s), 0), pipeline_mode=mode)
    vt_spec = lambda tile, mode=None: pl.BlockSpec((1, d, te), lambda i, s: (tile(s), 0, 0),
                                                   pipeline_mode=mode)
    a_blocks = pltpu.VMEM((te // N_KEYS, n_groups, N_KEYS, lane_w), F32)
    return pl.pallas_call(
        _peer_kernel,
        grid=(n // tm, n_steps),
        in_specs=[pl.BlockSpec((d, tm), lambda i, s: (0, i)),
                  u_spec(lambda s: 0, once),
                  u_spec(lambda s: 2 * s + 1),
                  u_spec(lambda s: jnp.minimum(2 * s + 2, n_tiles - 1)),
                  vt_spec(lambda s: jnp.maximum(2 * s - 1, 0)),
                  vt_spec(lambda s: 2 * s),
                  vt_spec(lambda s: n_tiles - 1, once),
                  table, table, table, table,
                  pl.BlockSpec((tm, d), lambda i, s: (i, 0)),
                  pl.BlockSpec((1, d), lambda i, s: (0, 0))],
        out_specs=pl.BlockSpec((tm, d), lambda i, s: (i, 0)),
        out_shape=jax.ShapeDtypeStruct((n, d), F32),
        scratch_shapes=[pltpu.VMEM((d, tm), F32), a_blocks, a_blocks,
                        pltpu.VMEM((te, tm), BF16), pltpu.VMEM((te, tm), BF16),
                        pltpu.VMEM(table_block, BF16), pltpu.VMEM(table_block, BF16)],
        compiler_params=_params(("parallel", "arbitrary"), PEER_VMEM_LIMIT),
        name="peer_dense",
    )(xt, u_bf, u_bf, u_bf, vt_tiles, vt_tiles, vt_tiles, *tables, h2, g_final)


def _t5_bias(rel, table):
    n = jnp.maximum(rel, 0)
    max_exact = N_BUCKETS // 2
    nf = jnp.maximum(n, 1).astype(F32)
    large = max_exact + (jnp.log(nf / max_exact) / math.log(MAX_DISTANCE / max_exact)
                         * (N_BUCKETS - max_exact)).astype(jnp.int32)
    large = jnp.minimum(large, N_BUCKETS - 1)
    bucket = jnp.where(n < max_exact, n, large)
    return jnp.moveaxis(table[bucket].astype(F32), -1, 0)


def _toeplitz(vec, t):
    hh = vec.shape[0]
    flat = jnp.tile(vec, (1, t))[:, :t * (2 * t - 1)]
    return flat.reshape(hh, t, 2 * t - 1)[:, :, :t]


def _prompt_bias_tiles(rel_bias, t):
    k = jnp.arange(2 * t)
    tiles = []
    for d in (0, 1):
        rel = jnp.where(k < t, d * t - k, d * t + 2 * t - k)
        vec = _t5_bias(rel, rel_bias)
        vec = jnp.where(rel[None, :] >= 0, vec, NEG_INF)
        tiles.append(_toeplitz(vec, t))
    return jnp.stack(tiles, axis=1)


def _rotary_tables(pos):
    half = RET_DK // 2
    inv = 1.0 / (ROPE_BASE ** (jnp.arange(half, dtype=F32) / half))
    ang = pos.astype(F32)[:, None] * inv[None, :]
    cos, sin = jnp.cos(ang), jnp.sin(ang)
    return jnp.concatenate([cos, cos], axis=1), jnp.concatenate([-sin, sin], axis=1)


def _retention_tables(log_gamma):
    n = jnp.arange(RET_CHUNK, dtype=F32)
    diff = n[:, None] - n[None, :]
    decay = jnp.where(diff[None] >= 0,
                      jnp.exp(jnp.maximum(diff, 0.0)[None] * log_gamma[:, None, None]), 0.0)
    ones = jnp.ones((1, 1, HEAD_W), F32)
    qdec = jnp.exp((n + 1.0)[None, :] * log_gamma[:, None])[:, :, None] * ones
    kdec = jnp.exp((RET_CHUNK - 1.0 - n)[None, :] * log_gamma[:, None])[:, :, None] * ones
    return decay, qdec, kdec, jnp.exp(RET_CHUNK * log_gamma)


def kernel(x_prompt, x_sample, mem_prompt, cache_diff_k, cache_diff_v, page_table, state_ret,
           cache_mem_k, cache_mem_v, g_mix, w_in, w_out, lam_q1, lam_k1, lam_q2, lam_k2, rel_bias,
           g_cross, g_mem, w_mq, w_mk, w_mv, w_mo, g_ffn, w_pq, sub_keys, peer_u, peer_v, g_final):
    depth = g_mix.shape[0]
    assert depth == 1
    l = 0
    b, s, d = x_prompt.shape
    db, t_s, _ = x_sample.shape
    assert t_s == 1
    n_pages = page_table.shape[1]
    past = n_pages * PAGE
    mem_len = mem_prompt.shape[1]
    n_tok = b * s

    log_gamma = jnp.log(1.0 - 2.0 ** (-5.0 - jnp.arange(RET_HEADS, dtype=F32)))
    lam_init = 0.8 - 0.6 * math.exp(-0.3 * l)
    lam = (jnp.exp(jnp.sum(lam_q1[l].astype(F32) * lam_k1[l].astype(F32)))
           - jnp.exp(jnp.sum(lam_q2[l].astype(F32) * lam_k2[l].astype(F32))) + lam_init)
    out_scale = 1.0 - lam_init

    row = lambda v: v.reshape(1, -1)
    w_in_bf = w_in[l].astype(BF16)
    w_out_bf = w_out[l].astype(BF16)
    w_mq_bf = w_mq[l].astype(BF16)
    w_mk_bf = w_mk[l].astype(BF16)
    w_mv_bf = w_mv[l].astype(BF16)
    w_mo_bf = w_mo[l].astype(BF16)
    w_pq_t = w_pq[l].T.astype(BF16)
    keys_bf = sub_keys[l].astype(BF16)
    u_bf = peer_u[l].astype(BF16)
    n_experts = peer_v.shape[1]
    vt_bf = jnp.transpose(peer_v[l].astype(BF16).reshape(n_experts // PEER_EXPERT_TILE, PEER_EXPERT_TILE, d),
                          (0, 2, 1))
    g_fin = row(g_final)

    bias_far = _t5_bias(jnp.full((1,), MAX_DISTANCE, jnp.int32), rel_bias)[:, 0]
    bias_zero = _t5_bias(jnp.zeros((1,), jnp.int32), rel_bias)[:, 0]
    scalars = jnp.concatenate([bias_far, lam.reshape(1), bias_zero]).astype(F32)

    cs_p, sn_p = _rotary_tables(jnp.arange(s))
    rq, rk, rv, rg, dqb, dk, dv, dkb, dvb = _in_proj(
        x_prompt.reshape(n_tok, d), row(g_mix[l]), w_in_bf, cs_p, sn_p, TOKEN_TILE)
    decay, qdec, kdec, gl = _retention_tables(log_gamma)
    shp = (b, s, GROUP_W)
    ret_p, state_p = _retention_prompt(rq.reshape(shp), rk.reshape(shp), rv.reshape(shp),
                                       rg.reshape(shp), decay, qdec, kdec, gl)
    dif_p = _diff_prompt(dqb.reshape(shp), dkb.reshape(shp), dvb.reshape(shp),
                         _prompt_bias_tiles(rel_bias, ATTN_TILE), scalars, out_scale)
    mk, mv, mk_bf, mv_bf = _mem_kv(mem_prompt.reshape(b * mem_len, d), row(g_mem[l]),
                                   w_mk_bf, w_mv_bf, mem_len)
    h2_p, xt_p = _post_mix(x_prompt.reshape(n_tok, d), ret_p.reshape(n_tok, GROUP_W),
                           dif_p.reshape(n_tok, GROUP_W), mk_bf.reshape(b, mem_len, d),
                           mv_bf.reshape(b, mem_len, d), w_out_bf, row(g_cross[l]), w_mq_bf,
                           w_mo_bf, row(g_ffn[l]), TOKEN_TILE)
    tables_p = _route(xt_p, w_pq_t, keys_bf)
    y_p = _peer(xt_p, u_bf, vt_bf, tables_p, h2_p, g_fin, PEER_TOKEN_TILE)

    lanes = SAMPLE_LANES
    assert db <= lanes
    cs_s, sn_s = _rotary_tables(jnp.full((db,), past, jnp.int32))
    xs = x_sample.reshape(db, d)
    rq_s, rk_s, rv_s, rg_s, dqb_s, dk_s, dv_s, _, _ = _in_proj(
        xs, row(g_mix[l]), w_in_bf, cs_s, sn_s, db)
    tok = (db, 1, GROUP_W)
    ret_s, state_s = _retention_sample(rq_s.reshape(tok), rk_s.reshape(tok), rv_s.reshape(tok),
                                       rg_s.reshape(tok), state_ret[l], jnp.exp(log_gamma))
    near = _t5_bias(jnp.arange(MAX_DISTANCE, 0, -1), rel_bias)
    far = jnp.broadcast_to(bias_far[:, None], (DIFF_HEADS, past - MAX_DISTANCE))
    bias_past = jnp.repeat(jnp.concatenate([far, near], axis=1), 2, axis=0)
    dif_s = _diff_sample(dqb_s.astype(F32).reshape(tok), dk_s.reshape(tok), dv_s.reshape(tok),
                         jnp.transpose(cache_diff_k[l], (0, 2, 3, 4, 1)).reshape(-1, GROUP_W, PAGE),
                         cache_diff_v[l].reshape(-1, PAGE * DIFF_HEADS, HEAD_W),
                         page_table, bias_past, scalars, out_scale)
    h2_s, xt_s = _post_mix_sample(xs, ret_s.reshape(db, GROUP_W), dif_s.reshape(db, GROUP_W),
                                  cache_mem_k[l].reshape(db, mem_len, d),
                                  cache_mem_v[l].reshape(db, mem_len, d),
                                  w_out_bf, row(g_cross[l]), w_mq_bf, w_mo_bf, row(g_ffn[l]), lanes)
    tables_s = _route(xt_s, w_pq_t, keys_bf)
    h2_pad = jnp.concatenate([h2_s, jnp.zeros((lanes - db, d), F32)], axis=0)
    y_s = _peer(xt_s, u_bf, vt_bf, tables_s, h2_pad, g_fin, lanes)[:db]

    return (y_p.reshape(b, s, d), y_s.reshape(db, 1, d),
            dk.reshape(1, b, s, DIFF_HEADS, 2, DIFF_DH), dv.reshape(1, b, s, DIFF_HEADS, HEAD_W),
            state_p[None], mk.reshape(1, b, mem_len, MEM_HEADS, d // MEM_HEADS),
            mv.reshape(1, b, mem_len, MEM_HEADS, d // MEM_HEADS),
            dk_s.reshape(1, db, 1, DIFF_HEADS, 2, DIFF_DH), dv_s.reshape(1, db, 1, DIFF_HEADS, HEAD_W),
            state_s[None])
```
